```python
import functools
import jax, jax.numpy as jnp
from jax import lax
import numpy as np

D_MODEL = 1024
BATCH = 16
SEQ = 2048
DEPTH = 1
DEC_BATCH = 128
DEC_SEQ = 1
PAST_LEN = 8192
PAGE_SIZE = 128

N_HEADS = 8
HEAD_DIM = 64
ATT_WIDTH = N_HEADS * HEAD_DIM
MOBA_BLOCK = 256
MOBA_TOPK = 3
Q_CHUNK = 64
CONV_CH = D_MODEL // 2
CONV_WIDTH = 31
PEER_HEADS = 8
N_KEYS = 128
N_EXPERTS = N_KEYS * N_KEYS
PEER_KEY_DIM = 128
PEER_TOPK = 16
PEER_CHUNK = 256
PLE_DIM = 256
PROJ_WIDTH = 3 * ATT_WIDTH + 2 * CONV_CH + 2 * D_MODEL
PROJ_SPLITS = (ATT_WIDTH, 2 * ATT_WIDTH, 3 * ATT_WIDTH, 3 * ATT_WIDTH + 2 * CONV_CH, 3 * ATT_WIDTH + 2 * CONV_CH + D_MODEL)
EPS = 1e-6
NEG = -1e30

kernel_name = 'moba_conformer_peer_hybrid_step'


def _rmsnorm(x, g):
    xf = x.astype(jnp.float32)
    y = xf * lax.rsqrt(jnp.mean(xf * xf, axis=-1, keepdims=True) + EPS)
    return (y * g.astype(jnp.float32)).astype(x.dtype)


def _layernorm(x, g, b):
    xf = x.astype(jnp.float32)
    mu = jnp.mean(xf, axis=-1, keepdims=True)
    var = jnp.mean(jnp.square(xf - mu), axis=-1, keepdims=True)
    y = (xf - mu) * lax.rsqrt(var + EPS) * g.astype(jnp.float32) + b.astype(jnp.float32)
    return y.astype(x.dtype)


def _moba_core(q, qpos, kmean, fetch):
    b, nq, h, _ = q.shape
    nb = kmean.shape[1]
    cur = qpos // MOBA_BLOCK
    gate = jnp.einsum('bqhd,bnhd->bqhn', q, kmean).astype(jnp.float32)
    past = jnp.arange(nb)[None, :] < cur[:, None]
    gate = jnp.where(past[None, :, None, :], gate, NEG)
    if nb < MOBA_TOPK:
        gate = jnp.pad(gate, ((0, 0), (0, 0), (0, 0), (0, MOBA_TOPK - nb)), constant_values=NEG)
    _, sel = lax.top_k(gate, MOBA_TOPK)
    own = jnp.broadcast_to(cur[None, :, None, None], (b, nq, h, 1)).astype(sel.dtype)
    blocks = jnp.concatenate([sel, own], axis=-1)
    blk_ok = jnp.concatenate([sel < cur[None, :, None, None], jnp.ones((b, nq, h, 1), bool)], axis=-1)
    kpos = blocks[..., None] * MOBA_BLOCK + jnp.arange(MOBA_BLOCK, dtype=blocks.dtype)
    ok = blk_ok[..., None] & (kpos <= qpos[None, :, None, None, None])
    kg, vg = fetch(kpos)
    s = jnp.einsum('bqhd,bqhnkd->bqhnk', q, kg).astype(jnp.float32) * (HEAD_DIM ** -0.5)
    s = jnp.where(ok, s, NEG)
    p = jax.nn.softmax(s.reshape(b, nq, h, -1), axis=-1).reshape(s.shape)
    return jnp.einsum('bqhnk,bqhnkd->bqhd', p.astype(vg.dtype), vg)


def _moba_prompt(q, k, v):
    b, s, h, dh = q.shape
    nb = -(-s // MOBA_BLOCK)
    kpad = jnp.pad(k, ((0, 0), (0, nb * MOBA_BLOCK - s), (0, 0), (0, 0)))
    kmean = (kpad.reshape(b, nb, MOBA_BLOCK, h, dh).astype(jnp.float32).sum(2) / MOBA_BLOCK).astype(q.dtype)
    bi = jnp.arange(b)[:, None, None, None, None]
    hi = jnp.arange(h)[None, None, :, None, None]

    def fetch(kpos):
        pc = jnp.clip(kpos, 0, s - 1)
        return k[bi, pc, hi], v[bi, pc, hi]

    nc = s // Q_CHUNK
    qc = q.reshape(b, nc, Q_CHUNK, h, dh).transpose(1, 0, 2, 3, 4)
    pos = jnp.arange(s, dtype=jnp.int32).reshape(nc, Q_CHUNK)
    out = lax.map(lambda a: _moba_core(a[0], a[1], kmean, fetch), (qc, pos))
    return out.transpose(1, 0, 2, 3, 4).reshape(b, s, h, dh)


def _moba_sample(q, k, v, cache_k, cache_v, ksum_pages, page_table, layer):
    db, t, h, dh = q.shape
    n_pages = PAST_LEN // PAGE_SIZE
    ppb = MOBA_BLOCK // PAGE_SIZE
    nb = -(-(PAST_LEN + t) // MOBA_BLOCK)
    ps = ksum_pages[page_table]
    ps = jnp.pad(ps, ((0, 0), (0, nb * ppb - n_pages), (0, 0), (0, 0)))
    bsum = ps.reshape(db, nb, ppb, h, dh).sum(2)
    qpos = PAST_LEN + jnp.arange(t, dtype=jnp.int32)
    bsum = bsum.at[:, qpos // MOBA_BLOCK].add(k.astype(jnp.float32))
    kmean = (bsum / MOBA_BLOCK).astype(q.dtype)
    bi = jnp.arange(db)[:, None, None, None, None]
    hi = jnp.arange(h)[None, None, :, None, None]

    def fetch(kpos):
        page = jnp.clip(kpos // PAGE_SIZE, 0, n_pages - 1)
        phys = page_table[bi, page]
        off = kpos % PAGE_SIZE
        new_i = jnp.clip(kpos - PAST_LEN, 0, t - 1)
        is_new = (kpos >= PAST_LEN)[..., None]
        kk = jnp.where(is_new, k[bi, new_i, hi], cache_k[layer, phys, off, hi])
        vv = jnp.where(is_new, v[bi, new_i, hi], cache_v[layer, phys, off, hi])
        return kk, vv

    return _moba_core(q, qpos, kmean, fetch)


def _conv_module(glu_in, state, dw_w, dw_b, ln_g, ln_b):
    a, g = jnp.split(glu_in, 2, axis=-1)
    z = a * jax.nn.sigmoid(g)
    if state is None:
        zp = jnp.pad(z, ((0, 0), (CONV_WIDTH - 1, 0), (0, 0)))
    else:
        zp = jnp.concatenate([state.astype(z.dtype), z], axis=1)
    y = lax.conv_general_dilated(zp, dw_w[:, None, :].astype(z.dtype), window_strides=(1,), padding='VALID',
                                 dimension_numbers=('NWC', 'WIO', 'NWC'), feature_group_count=CONV_CH) + dw_b
    y = jax.nn.silu(_layernorm(y, ln_g, ln_b))
    return y, zp[:, -(CONV_WIDTH - 1):]


def _peer(hn, wq, k1, k2, u_tab, v_tab):
    b, t, d = hn.shape
    n = b * t
    n_pad = -(-n // PEER_CHUNK) * PEER_CHUNK
    toks = jnp.pad(hn.reshape(n, d), ((0, n_pad - n), (0, 0))).reshape(n_pad // PEER_CHUNK, PEER_CHUNK, d)

    def chunk(tc):
        q = (tc @ wq).reshape(PEER_CHUNK, PEER_HEADS, 2, PEER_KEY_DIM // 2)
        s1 = jnp.einsum('thd,kd->thk', q[:, :, 0], k1).astype(jnp.float32)
        s2 = jnp.einsum('thd,kd->thk', q[:, :, 1], k2).astype(jnp.float32)
        v1, i1 = lax.top_k(s1, PEER_TOPK)
        v2, i2 = lax.top_k(s2, PEER_TOPK)
        cand = (v1[..., :, None] + v2[..., None, :]).reshape(PEER_CHUNK, PEER_HEADS, PEER_TOPK * PEER_TOPK)
        cidx = (i1[..., :, None] * N_KEYS + i2[..., None, :]).reshape(PEER_CHUNK, PEER_HEADS, PEER_TOPK * PEER_TOPK)
        sc, pick = lax.top_k(cand, PEER_TOPK)
        eidx = jnp.take_along_axis(cidx, pick, axis=-1)
        g = jax.nn.softmax(sc, axis=-1)
        act = jax.nn.gelu(jnp.einsum('td,thkd->thk', tc, u_tab[eidx]).astype(jnp.float32), approximate=False) * g
        return jnp.einsum('thk,thkd->td', act.astype(tc.dtype), v_tab[eidx])

    out = lax.map(chunk, toks).reshape(n_pad, d)[:n]
    return out.reshape(b, t, d)


def _layer(x, pe, attend, conv_state, norm1_g, w_in, dw_w, dw_b, cln_g, cln_b, w_att_out, w_conv_out, w_o,
           norm2_g, peer_wq, peer_k1, peer_k2, peer_u, peer_v, norm3_g, w_ple, w_ple_gate):
    b, t, _ = x.shape
    h = _rmsnorm(x, norm1_g)
    q, k, v, glu, ga, gb = jnp.split(h @ w_in, PROJ_SPLITS, axis=-1)
    q = q.reshape(b, t, N_HEADS, HEAD_DIM)
    k = k.reshape(b, t, N_HEADS, HEAD_DIM)
    v = v.reshape(b, t, N_HEADS, HEAD_DIM)
    att = attend(q, k, v).reshape(b, t, ATT_WIDTH)
    conv, new_conv = _conv_module(glu, conv_state, dw_w, dw_b, cln_g, cln_b)
    merged = jax.nn.sigmoid(ga) * (att @ w_att_out) + jax.nn.sigmoid(gb) * (conv @ w_conv_out)
    x = x + merged @ w_o
    x = x + _peer(_rmsnorm(x, norm2_g), peer_wq, peer_k1, peer_k2, peer_u, peer_v)
    x = x + jax.nn.sigmoid(_rmsnorm(x, norm3_g) @ w_ple_gate) * (pe @ w_ple)
    return x, k, v, new_conv


def setup_inputs(seed: int = 0) -> dict:
    key = jax.random.key(seed)
    ks = jax.random.split(key, 27)
    f32 = jnp.float32
    n_pages = PAST_LEN // PAGE_SIZE
    n_phys = (DEC_BATCH * n_pages * 5) // 4

    def nrm(k, shape, scale):
        return jax.random.normal(k, shape, f32) * scale

    def gain(k, shape):
        return 1.0 + 0.01 * jax.random.normal(k, shape, f32)

    page_table = jax.random.permutation(ks[5], n_phys)[:DEC_BATCH * n_pages].reshape(DEC_BATCH, n_pages).astype(jnp.int32)
    return {
        'x_prompt': nrm(ks[0], (BATCH, SEQ, D_MODEL), 1.0),
        'x_sample': nrm(ks[1], (DEC_BATCH, DEC_SEQ, D_MODEL), 1.0),
        'cache_k': nrm(ks[2], (DEPTH, n_phys, PAGE_SIZE, N_HEADS, HEAD_DIM), 1.0),
        'cache_v': nrm(ks[3], (DEPTH, n_phys, PAGE_SIZE, N_HEADS, HEAD_DIM), 1.0),
        'state_conv': nrm(ks[4], (DEPTH, DEC_BATCH, CONV_WIDTH - 1, CONV_CH), 0.5),
        'page_table': page_table,
        'p_prompt': nrm(ks[6], (DEPTH, BATCH, SEQ, PLE_DIM), 1.0),
        'p_sample': nrm(ks[7], (DEPTH, DEC_BATCH, DEC_SEQ, PLE_DIM), 1.0),
        'norm1_g': gain(ks[8], (DEPTH, D_MODEL)),
        'w_in': nrm(ks[9], (DEPTH, D_MODEL, PROJ_WIDTH), D_MODEL ** -0.5),
        'dw_w': nrm(ks[10], (DEPTH, CONV_WIDTH, CONV_CH), CONV_WIDTH ** -0.5),
        'dw_b': nrm(ks[11], (DEPTH, CONV_CH), 0.01),
        'cln_g': gain(ks[12], (DEPTH, CONV_CH)),
        'cln_b': nrm(ks[13], (DEPTH, CONV_CH), 0.01),
        'w_att_out': nrm(ks[14], (DEPTH, ATT_WIDTH, D_MODEL), ATT_WIDTH ** -0.5),
        'w_conv_out': nrm(ks[15], (DEPTH, CONV_CH, D_MODEL), CONV_CH ** -0.5),
        'w_o': nrm(ks[16], (DEPTH, D_MODEL, D_MODEL), D_MODEL ** -0.5),
        'norm2_g': gain(ks[17], (DEPTH, D_MODEL)),
        'peer_wq': nrm(ks[18], (DEPTH, D_MODEL, PEER_HEADS * PEER_KEY_DIM), D_MODEL ** -0.5),
        'peer_k1': nrm(ks[19], (DEPTH, N_KEYS, PEER_KEY_DIM // 2), (PEER_KEY_DIM // 2) ** -0.5),
        'peer_k2': nrm(ks[20], (DEPTH, N_KEYS, PEER_KEY_DIM // 2), (PEER_KEY_DIM // 2) ** -0.5),
        'peer_u': nrm(ks[21], (DEPTH, N_EXPERTS, D_MODEL), D_MODEL ** -0.5),
        'peer_v': nrm(ks[22], (DEPTH, N_EXPERTS, D_MODEL), 0.5),
        'norm3_g': gain(ks[23], (DEPTH, D_MODEL)),
        'w_ple': nrm(ks[24], (DEPTH, PLE_DIM, D_MODEL), PLE_DIM ** -0.5),
        'w_ple_gate': nrm(ks[25], (DEPTH, D_MODEL, D_MODEL), D_MODEL ** -0.5),
        'final_g': gain(ks[26], (D_MODEL,)),
    }


def reference(x_prompt, x_sample, cache_k, cache_v, state_conv, page_table, p_prompt, p_sample,
              norm1_g, w_in, dw_w, dw_b, cln_g, cln_b, w_att_out, w_conv_out, w_o, norm2_g,
              peer_wq, peer_k1, peer_k2, peer_u, peer_v, norm3_g, w_ple, w_ple_gate, final_g):
    page_ksum = jnp.sum(cache_k, axis=2, dtype=jnp.float32)
    xp, xs = x_prompt, x_sample
    kp_l, vp_l, cp_l, ks_l, vs_l, cs_l = [], [], [], [], [], []
    for i in range(DEPTH):
        lw = (norm1_g[i], w_in[i], dw_w[i], dw_b[i], cln_g[i], cln_b[i], w_att_out[i], w_conv_out[i], w_o[i],
              norm2_g[i], peer_wq[i], peer_k1[i], peer_k2[i], peer_u[i], peer_v[i], norm3_g[i], w_ple[i], w_ple_gate[i])
        xp, kp, vp, cp = _layer(xp, p_prompt[i], _moba_prompt, None, *lw)
        attend_s = functools.partial(_moba_sample, cache_k=cache_k, cache_v=cache_v, ksum_pages=page_ksum[i],
                                     page_table=page_table, layer=i)
        xs, ks, vs, cs = _layer(xs, p_sample[i], attend_s, state_conv[i], *lw)
        kp_l.append(kp); vp_l.append(vp); cp_l.append(cp)
        ks_l.append(ks); vs_l.append(vs); cs_l.append(cs)
    y_prompt = _rmsnorm(xp, final_g)
    y_sample = _rmsnorm(xs, final_g)
    return (y_prompt, y_sample, jnp.stack(kp_l), jnp.stack(vp_l), jnp.stack(cp_l), jnp.stack(ks_l), jnp.stack(vs_l), jnp.stack(cs_l))
```

```python
import functools

import numpy as np
import jax
import jax.numpy as jnp
from jax import lax
from jax.experimental import pallas as pl
from jax.experimental.pallas import tpu as pltpu

F32 = jnp.float32
BF16 = jnp.bfloat16
I32 = jnp.int32

N_HEADS = 8
HEAD_DIM = 64
MOBA_BLOCK = 256
MOBA_TOPK = 3
PEER_HEADS = 8
PEER_TOPK = 16
EPS = 1e-6
NEG = -1e30
ATT_SCALE = HEAD_DIM ** -0.5

LANES = 128
SUBLANES = 8
VMEM_LIMIT_BYTES = 56 * 1024 * 1024

HIGHEST = lax.Precision.HIGHEST


def _cparams(sem):
    return pltpu.CompilerParams(dimension_semantics=sem, vmem_limit_bytes=VMEM_LIMIT_BYTES)


def _rms(x, g):
    return x * lax.rsqrt(jnp.mean(x * x, axis=-1, keepdims=True) + EPS) * g


def _sigmoid(x):
    return 1.0 / (1.0 + jnp.exp(-x))


def _dot(a, b):
    return jnp.dot(a.astype(BF16), b.astype(BF16), preferred_element_type=F32)


def _dot_nt(a, b, precision=None):
    return lax.dot_general(a, b, (((1,), (1,)), ((), ())), preferred_element_type=F32, precision=precision)


def _gelu_erf(x):
    return 0.5 * x * (1.0 + lax.erf(x * np.float32(1.0 / np.sqrt(2.0))))


def _inproj_kernel(x_ref, g_ref, w_ref, q_ref, k_ref, v_ref, z_ref, sga_ref, sgb_ref, *, att_w, conv_ch, d_model):
    h = _rms(x_ref[...], g_ref[...]).astype(BF16)

    def proj(lo, width):
        return jnp.dot(h, w_ref[:, lo:lo + width], preferred_element_type=F32)

    q_ref[...] = proj(0, att_w)
    k_ref[...] = proj(att_w, att_w)
    v_ref[...] = proj(2 * att_w, att_w)
    base = 3 * att_w
    z_ref[...] = proj(base, conv_ch) * _sigmoid(proj(base + conv_ch, conv_ch))
    base += 2 * conv_ch
    sga_ref[...] = _sigmoid(proj(base, d_model))
    sgb_ref[...] = _sigmoid(proj(base + d_model, d_model))


def _inproj(x2, g, w_bf, tm):
    t, d = x2.shape
    att_w = N_HEADS * HEAD_DIM
    conv_ch = d // 2
    assert w_bf.shape[1] == 3 * att_w + 2 * conv_ch + 2 * d and t % tm == 0
    row = lambda w: pl.BlockSpec((tm, w), lambda i: (i, 0))
    full = lambda a: pl.BlockSpec(a.shape, lambda i: (0,) * a.ndim)
    out_w = (att_w, att_w, att_w, conv_ch, d, d)
    return pl.pallas_call(
        functools.partial(_inproj_kernel, att_w=att_w, conv_ch=conv_ch, d_model=d),
        grid=(t // tm,),
        in_specs=[row(d), full(g), full(w_bf)],
        out_specs=[row(w) for w in out_w],
        out_shape=[jax.ShapeDtypeStruct((t, w), F32) for w in out_w],
        compiler_params=_cparams(("parallel",)),
        name="inproj",
    )(x2, g, w_bf)


def _moba_prompt_kernel(q_ref, k_ref, v_ref, o_ref, kexp_ref, sel_ref, *, nb):
    blk = MOBA_BLOCK
    i = pl.program_id(2)

    @pl.when(i == 0)
    def _():
        for j in range(nb):
            km = jnp.sum(k_ref[0, j * blk:(j + 1) * blk, :], axis=0, keepdims=True) * (1.0 / blk)
            kexp_ref[j * LANES:(j + 1) * LANES, :] = jnp.broadcast_to(km, (LANES, LANES))

    lane = lax.broadcasted_iota(I32, (1, LANES), 1)
    row_i = lax.broadcasted_iota(I32, (blk, blk), 0)
    col_i = lax.broadcasted_iota(I32, (blk, blk), 1)
    causal = col_i <= row_i
    kexp = kexp_ref[...]
    outs = []
    for hs in range(2):
        hm = (lane < HEAD_DIM) if hs == 0 else (lane >= HEAD_DIM)
        q = jnp.where(hm, q_ref[0], 0.0)
        gate_all = _dot_nt(q, kexp, HIGHEST)
        gates = [jnp.where(j < i, gate_all[:, j * LANES:(j + 1) * LANES], NEG) for j in range(nb)]
        for j in range(nb):
            rank = jnp.zeros((blk, LANES), F32)
            for m in range(nb):
                if m == j:
                    continue
                ahead = (gates[m] >= gates[j]) if m < j else (gates[m] > gates[j])
                rank = rank + ahead.astype(F32)
            sel_ref[hs, j] = jnp.where((rank < MOBA_TOPK) & (j < i), 1.0, 0.0)

        qb = q.astype(BF16)

        def step(s, mask, vj, carry):
            m_run, l_run, acc = carry
            s = jnp.where(mask, s * ATT_SCALE, NEG)
            m_new = jnp.maximum(m_run, jnp.max(s, axis=-1, keepdims=True))
            p = jnp.where(mask, jnp.exp(s - m_new), 0.0)
            alpha = jnp.exp(m_run - m_new)
            l_new = alpha * l_run + jnp.sum(p, axis=-1, keepdims=True)
            acc_new = alpha * acc + _dot(p, vj)
            return m_new, l_new, acc_new

        def past_body(j, carry):
            start = pl.multiple_of(j * blk, blk)
            kj = k_ref[0, pl.ds(start, blk), :]
            vj = v_ref[0, pl.ds(start, blk), :]
            s = _dot_nt(qb, kj.astype(BF16))
            sj = sel_ref[hs, j]
            mask = jnp.concatenate([sj, sj], axis=1) > 0.5
            return step(s, mask, vj, carry)

        init = (jnp.full((blk, 1), NEG, F32), jnp.zeros((blk, 1), F32), jnp.zeros((blk, LANES), F32))
        carry = lax.fori_loop(0, i, past_body, init)
        start = pl.multiple_of(i * blk, blk)
        kd = k_ref[0, pl.ds(start, blk), :]
        vd = v_ref[0, pl.ds(start, blk), :]
        _, l_fin, acc = step(_dot_nt(qb, kd.astype(BF16)), causal, vd, carry)
        outs.append(acc / l_fin)
    o_ref[0] = jnp.where(lane < HEAD_DIM, outs[0], outs[1])


def _moba_prompt(q, k, v):
    b, s, w = q.shape
    blk = MOBA_BLOCK
    nb = s // blk
    assert s % blk == 0 and nb >= MOBA_TOPK and w % LANES == 0 and LANES == 2 * HEAD_DIM
    kv_spec = pl.BlockSpec((1, s, LANES), lambda bi, hp, i: (bi, 0, hp))
    q_spec = pl.BlockSpec((1, blk, LANES), lambda bi, hp, i: (bi, i, hp))
    return pl.pallas_call(
        functools.partial(_moba_prompt_kernel, nb=nb),
        grid=(b, w // LANES, nb),
        in_specs=[q_spec, kv_spec, kv_spec],
        out_specs=q_spec,
        out_shape=jax.ShapeDtypeStruct((b, s, w), F32),
        scratch_shapes=[pltpu.VMEM((nb * LANES, LANES), F32), pltpu.VMEM((2, nb, blk, LANES), F32)],
        compiler_params=_cparams(("parallel", "parallel", "arbitrary")),
        name="moba_prompt",
    )(q, k, v)


PAGES_PER_STEP = 8


def _head_rows(row):
    w = row.shape[-1]
    lane_head = lax.broadcasted_iota(I32, (N_HEADS, w), 1) // HEAD_DIM
    sub = lax.broadcasted_iota(I32, (N_HEADS, w), 0)
    return jnp.where(lane_head == sub, jnp.broadcast_to(row, (N_HEADS, w)), 0.0)


def _sample_scores_kernel(pt_ref, q_ref, *refs, page, pages_per_block):
    del pt_ref
    k_refs = refs[:PAGES_PER_STEP]
    s_ref, ksum_ref = refs[PAGES_PER_STEP:]
    qbd = _head_rows(q_ref[0]).astype(BF16)
    sums = []
    for p in range(PAGES_PER_STEP):
        kp = k_refs[p][0]
        s_ref[0, :, p * page:(p + 1) * page] = _dot_nt(qbd, kp.astype(BF16)) * ATT_SCALE
        sums.append(jnp.sum(kp, axis=0, keepdims=True))
    for bi in range(PAGES_PER_STEP // pages_per_block):
        tot = sums[bi * pages_per_block]
        for r in range(1, pages_per_block):
            tot = tot + sums[bi * pages_per_block + r]
        ksum_ref[0, 0, bi:bi + 1, :] = tot


def _page_specs(n_pages, page, w):
    def spec(p):
        return pl.BlockSpec((1, page, w), lambda b, s, pt: (pt[b * n_pages + s * PAGES_PER_STEP + p], 0, 0))
    return [spec(p) for p in range(PAGES_PER_STEP)]


def _sample_scores(q, cache_k3, pt_flat, n_pages):
    db, w = q.shape
    page = cache_k3.shape[1]
    ppb = MOBA_BLOCK // page
    steps = n_pages // PAGES_PER_STEP
    bps = PAGES_PER_STEP // ppb
    assert n_pages % PAGES_PER_STEP == 0 and PAGES_PER_STEP % ppb == 0 and MOBA_BLOCK % page == 0
    grid_spec = pltpu.PrefetchScalarGridSpec(
        num_scalar_prefetch=1,
        grid=(db, steps),
        in_specs=[pl.BlockSpec((1, 1, w), lambda b, s, pt: (b, 0, 0))] + _page_specs(n_pages, page, w),
        out_specs=[pl.BlockSpec((1, N_HEADS, PAGES_PER_STEP * page), lambda b, s, pt: (b, 0, s)),
                   pl.BlockSpec((1, 1, bps, w), lambda b, s, pt: (b, s, 0, 0))],
    )
    return pl.pallas_call(
        functools.partial(_sample_scores_kernel, page=page, pages_per_block=ppb),
        grid_spec=grid_spec,
        out_shape=[jax.ShapeDtypeStruct((db, N_HEADS, n_pages * page), F32),
                   jax.ShapeDtypeStruct((db, steps, bps, w), F32)],
        compiler_params=_cparams(("parallel", "arbitrary")),
        name="sample_scores",
    )(pt_flat, q.reshape(db, 1, w), *([cache_k3] * PAGES_PER_STEP))


def _sample_attend_kernel(pt_ref, q_ref, kn_ref, vn_ref, s_ref, ksum_ref, *refs, page, nb, steps):
    del pt_ref
    v_refs = refs[:PAGES_PER_STEP]
    o_ref = refs[PAGES_PER_STEP]
    p_ref, acc_ref, l_ref, pn_ref = refs[PAGES_PER_STEP + 1:]
    st = pl.program_id(1)
    span = PAGES_PER_STEP * page
    qbd = _head_rows(q_ref[0])

    @pl.when(st == 0)
    def _():
        kmean = ksum_ref[0] * (1.0 / MOBA_BLOCK)
        gate = _dot_nt(qbd, kmean, HIGHEST)
        bidx = lax.broadcasted_iota(I32, (N_HEADS, nb), 1)
        sel = jnp.zeros((N_HEADS, nb), F32)
        for _ in range(MOBA_TOPK):
            mx = jnp.max(gate, axis=-1, keepdims=True)
            pick = jnp.min(jnp.where(gate == mx, bidx, nb), axis=-1, keepdims=True)
            hit = bidx == pick
            sel = jnp.where(hit, 1.0, sel)
            gate = jnp.where(hit, -jnp.inf, gate)
        n_keys = nb * MOBA_BLOCK
        expand = (lax.broadcasted_iota(I32, (nb, n_keys), 1) // MOBA_BLOCK
                  == lax.broadcasted_iota(I32, (nb, n_keys), 0))
        mask = _dot(sel, expand.astype(F32)) > 0.5
        s_all = s_ref[0]
        s_new = jnp.sum(qbd * kn_ref[0], axis=-1, keepdims=True) * ATT_SCALE
        m = jnp.maximum(jnp.max(jnp.where(mask, s_all, NEG), axis=-1, keepdims=True), s_new)
        p = jnp.where(mask, jnp.exp(s_all - m), 0.0)
        pn = jnp.exp(s_new - m)
        l_ref[...] = jnp.broadcast_to(jnp.sum(p, axis=-1, keepdims=True) + pn, l_ref.shape)
        pn_ref[...] = jnp.broadcast_to(pn, pn_ref.shape)
        for t in range(steps):
            p_ref[t] = p[:, t * span:(t + 1) * span]
        acc_ref[...] = jnp.zeros_like(acc_ref)

    p_step = p_ref[st]
    acc = acc_ref[...]
    for pg in range(PAGES_PER_STEP):
        acc = acc + _dot(p_step[:, pg * page:(pg + 1) * page], v_refs[pg][0])
    acc_ref[...] = acc

    @pl.when(st == steps - 1)
    def _():
        w = acc.shape[-1]
        full = (acc + pn_ref[:, 0:1] * vn_ref[0]) / l_ref[:, 0:1]
        lane_head = lax.broadcasted_iota(I32, (N_HEADS, w), 1) // HEAD_DIM
        sub = lax.broadcasted_iota(I32, (N_HEADS, w), 0)
        o_ref[0] = jnp.sum(jnp.where(lane_head == sub, full, 0.0), axis=0, keepdims=True)


def _sample_attend(q, k_new, v_new, scores, ksum, cache_v3, pt_flat, n_pages):
    db, w = q.shape
    page = cache_v3.shape[1]
    steps = n_pages // PAGES_PER_STEP
    nb = n_pages * page // MOBA_BLOCK
    assert nb >= MOBA_TOPK
    one = lambda: pl.BlockSpec((1, 1, w), lambda b, s, pt: (b, 0, 0))
    grid_spec = pltpu.PrefetchScalarGridSpec(
        num_scalar_prefetch=1,
        grid=(db, steps),
        in_specs=[one(), one(), one(),
                  pl.BlockSpec((1, N_HEADS, n_pages * page), lambda b, s, pt: (b, 0, 0)),
                  pl.BlockSpec((1, nb, w), lambda b, s, pt: (b, 0, 0))] + _page_specs(n_pages, page, w),
        out_specs=pl.BlockSpec((1, 1, w), lambda b, s, pt: (b, 0, 0)),
        scratch_shapes=[pltpu.VMEM((steps, N_HEADS, PAGES_PER_STEP * page), F32),
                        pltpu.VMEM((N_HEADS, w), F32),
                        pltpu.VMEM((N_HEADS, LANES), F32),
                        pltpu.VMEM((N_HEADS, LANES), F32)],
    )
    r3 = lambda a: a.reshape(db, 1, w)
    out = pl.pallas_call(
        functools.partial(_sample_attend_kernel, page=page, nb=nb, steps=steps),
        grid_spec=grid_spec,
        out_shape=jax.ShapeDtypeStruct((db, 1, w), F32),
        compiler_params=_cparams(("parallel", "arbitrary")),
        name="sample_attend",
    )(pt_flat, r3(q), r3(k_new), r3(v_new), scores, ksum.reshape(db, nb, w), *([cache_v3] * PAGES_PER_STEP))
    return out.reshape(db, w)


CONV_HALO = 32


def _ln_silu(y, g, b):
    mu = jnp.mean(y, axis=-1, keepdims=True)
    var = jnp.mean(jnp.square(y - mu), axis=-1, keepdims=True)
    n = (y - mu) * lax.rsqrt(var + EPS) * g + b
    return n * _sigmoid(n)


def _conv_prompt_kernel(zc_ref, zp_ref, dw_ref, db_ref, g_ref, b_ref, o_ref, win_ref, *, cw, tile):
    i = pl.program_id(1)
    prev = zp_ref[0, tile - CONV_HALO:tile, :]
    win_ref[0:CONV_HALO, :] = jnp.where(i > 0, prev, 0.0)
    win_ref[CONV_HALO:CONV_HALO + tile, :] = zc_ref[0]
    off = CONV_HALO - (cw - 1)
    y = jnp.broadcast_to(db_ref[...], (tile, db_ref.shape[-1]))
    for w in range(cw):
        y = y + win_ref[off + w:off + w + tile, :] * dw_ref[w:w + 1, :]
    o_ref[0] = _ln_silu(y, g_ref[...], b_ref[...])


def _conv_prompt(z, dw_w, dw_b, ln_g, ln_b, tile=256):
    b, s, c = z.shape
    cw = dw_w.shape[0]
    assert s % tile == 0 and cw - 1 <= CONV_HALO <= tile
    full = lambda a: pl.BlockSpec(a.shape, lambda bi, i: (0,) * a.ndim)
    return pl.pallas_call(
        functools.partial(_conv_prompt_kernel, cw=cw, tile=tile),
        grid=(b, s // tile),
        in_specs=[pl.BlockSpec((1, tile, c), lambda bi, i: (bi, i, 0)),
                  pl.BlockSpec((1, tile, c), lambda bi, i: (bi, jnp.maximum(i - 1, 0), 0)),
                  full(dw_w), full(dw_b), full(ln_g), full(ln_b)],
        out_specs=pl.BlockSpec((1, tile, c), lambda bi, i: (bi, i, 0)),
        out_shape=jax.ShapeDtypeStruct((b, s, c), F32),
        scratch_shapes=[pltpu.VMEM((CONV_HALO + tile, c), F32)],
        compiler_params=_cparams(("parallel", "arbitrary")),
        name="conv_prompt",
    )(z, z, dw_w, dw_b, ln_g, ln_b)


SAMPLE_CONV_ROWS = 8


def _conv_sample_kernel(st_ref, z_ref, dw_ref, db_ref, g_ref, b_ref, o_ref, ns_ref, *, cw):
    rows = []
    for r in range(SAMPLE_CONV_ROWS):
        y = jnp.sum(st_ref[r] * dw_ref[0:cw - 1, :], axis=0, keepdims=True)
        rows.append(y + z_ref[r:r + 1, :] * dw_ref[cw - 1:cw, :] + db_ref[...])
        ns_ref[r, 0:cw - 2, :] = st_ref[r, 1:cw - 1, :]
        ns_ref[r, cw - 2:cw - 1, :] = z_ref[r:r + 1, :]
    o_ref[...] = _ln_silu(jnp.concatenate(rows, axis=0), g_ref[...], b_ref[...])


def _conv_sample(state, z, dw_w, dw_b, ln_g, ln_b):
    db, sw, c = state.shape
    cw = dw_w.shape[0]
    r = SAMPLE_CONV_ROWS
    assert sw == cw - 1 and db % r == 0
    full = lambda a: pl.BlockSpec(a.shape, lambda i: (0,) * a.ndim)
    return pl.pallas_call(
        functools.partial(_conv_sample_kernel, cw=cw),
        grid=(db // r,),
        in_specs=[pl.BlockSpec((r, sw, c), lambda i: (i, 0, 0)), pl.BlockSpec((r, c), lambda i: (i, 0)),
                  full(dw_w), full(dw_b), full(ln_g), full(ln_b)],
        out_specs=[pl.BlockSpec((r, c), lambda i: (i, 0)), pl.BlockSpec((r, sw, c), lambda i: (i, 0, 0))],
        out_shape=[jax.ShapeDtypeStruct((db, c), F32), jax.ShapeDtypeStruct((db, sw, c), F32)],
        compiler_params=_cparams(("parallel",)),
        name="conv_sample",
    )(state, z, dw_w, dw_b, ln_g, ln_b)


def _outproj_kernel(x_ref, att_ref, conv_ref, sga_ref, sgb_ref, wao_ref, wco_ref, wo_ref, g2_ref, wq_ref,
                    x1_ref, hn_ref, pq_ref):
    merged = sga_ref[...] * _dot(att_ref[...], wao_ref[...]) + sgb_ref[...] * _dot(conv_ref[...], wco_ref[...])
    x1 = x_ref[...] + _dot(merged, wo_ref[...])
    x1_ref[...] = x1
    hn = _rms(x1, g2_ref[...])
    hn_ref[...] = hn
    pq_ref[...] = _dot(hn, wq_ref[...])


def _outproj(x2, att, conv, sga, sgb, wao, wco, wo, g2, wq, tm):
    t, d = x2.shape
    assert t % tm == 0
    row = lambda a: pl.BlockSpec((tm, a.shape[1]), lambda i: (i, 0))
    full = lambda a: pl.BlockSpec(a.shape, lambda i: (0,) * a.ndim)
    acts = (x2, att, conv, sga, sgb)
    wts = (wao, wco, wo, g2, wq)
    pq_w = wq.shape[1]
    return pl.pallas_call(
        _outproj_kernel,
        grid=(t // tm,),
        in_specs=[row(a) for a in acts] + [full(a) for a in wts],
        out_specs=[pl.BlockSpec((tm, d), lambda i: (i, 0)), pl.BlockSpec((tm, d), lambda i: (i, 0)),
                   pl.BlockSpec((tm, pq_w), lambda i: (i, 0))],
        out_shape=[jax.ShapeDtypeStruct((t, d), F32), jax.ShapeDtypeStruct((t, d), F32),
                   jax.ShapeDtypeStruct((t, pq_w), F32)],
        compiler_params=_cparams(("parallel",)),
        name="outproj",
    )(*acts, *wts)


PEER_TILE = 128


def _topk_rows(s, payload, k):
    n, t = s.shape
    ridx = lax.broadcasted_iota(I32, (n, t), 0)
    kidx = lax.broadcasted_iota(I32, (k, t), 0)

    def body(it, carry):
        s_cur, vals, pays = carry
        mx = jnp.max(s_cur, axis=0, keepdims=True)
        pick = jnp.min(jnp.where(s_cur == mx, ridx, n), axis=0, keepdims=True)
        hit = ridx == pick
        pay = jnp.max(jnp.where(hit, payload, -1), axis=0, keepdims=True)
        vals = jnp.where(kidx == it, mx, vals)
        pays = jnp.where(kidx == it, pay, pays)
        return jnp.where(hit, -jnp.inf, s_cur), vals, pays

    init = (s, jnp.zeros((k, t), F32), jnp.zeros((k, t), I32))
    _, vals, pays = lax.fori_loop(0, k, body, init)
    return vals, pays


def _peer_route_kernel(pq_ref, k1_ref, k2_ref, pidx_ref, shift_ref, gate_ref, *, n_keys):
    kk = PEER_TOPK
    t = PEER_TILE
    key_id = lax.broadcasted_iota(I32, (n_keys, t), 0)
    for h in range(PEER_HEADS):
        qh = pq_ref[:, h * LANES:(h + 1) * LANES]
        s1 = _dot_nt(k1_ref[...], qh, HIGHEST)
        s2 = _dot_nt(k2_ref[...], qh, HIGHEST)
        v1, i1 = _topk_rows(s1, key_id, kk)
        v2, i2 = _topk_rows(s2, key_id, kk)
        cand = jnp.concatenate([v1[a:a + 1, :] + v2 for a in range(kk)], axis=0)
        cidx = jnp.concatenate([i1[a:a + 1, :] * n_keys + i2 for a in range(kk)], axis=0)
        sc, eidx = _topk_rows(cand, cidx, kk)
        e = jnp.exp(sc - jnp.max(sc, axis=0, keepdims=True))
        gate_ref[0, h * kk:(h + 1) * kk, :] = e / jnp.sum(e, axis=0, keepdims=True)
        pidx_ref[0, h * kk:(h + 1) * kk, :] = eidx >> 1
        shift_ref[0, h * kk:(h + 1) * kk, :] = ((eidx & 1) ^ 1) << 4


def _peer_route(pq, k1ext, k2ext):
    t, w = pq.shape
    n_keys = k1ext.shape[0]
    slots = PEER_HEADS * PEER_TOPK
    assert t % PEER_TILE == 0 and w == PEER_HEADS * LANES and k1ext.shape[1] == LANES
    nt = t // PEER_TILE
    full = lambda a: pl.BlockSpec(a.shape, lambda i: (0,) * a.ndim)
    ospec = pl.BlockSpec((1, slots, PEER_TILE), lambda i: (i, 0, 0))
    return pl.pallas_call(
        functools.partial(_peer_route_kernel, n_keys=n_keys),
        grid=(nt,),
        in_specs=[pl.BlockSpec((PEER_TILE, w), lambda i: (i, 0)), full(k1ext), full(k2ext)],
        out_specs=[ospec, ospec, ospec],
        out_shape=[jax.ShapeDtypeStruct((nt, slots, PEER_TILE), I32), jax.ShapeDtypeStruct((nt, slots, PEER_TILE), I32),
                   jax.ShapeDtypeStruct((nt, slots, PEER_TILE), F32)],
        compiler_params=_cparams(("parallel",)),
        name="peer_route",
    )(pq, k1ext, k2ext)


HI_MASK = np.uint32(0xFFFF0000)


def _butterfly_order():
    ids = [np.full((SUBLANES,), i) for i in range(SUBLANES)]
    sub = np.arange(SUBLANES)
    for shift, keep in ((1, sub % 2 == 0), (2, sub % 4 < 2), (4, sub < 4)):
        nxt = []
        for a, b in zip(ids[0::2], ids[1::2]):
            first = np.where(keep, a, b)
            other = np.roll(np.where(keep, b, a), shift)
            assert (first == other).all()
            nxt.append(first)
        ids = nxt
    owner = ids[0]
    order = np.empty(SUBLANES, np.int64)
    order[owner] = np.arange(SUBLANES)
    return [int(v) for v in order]


_BUTTERFLY_ORDER = _butterfly_order()


def _sublane_sums(vs):
    sub = lax.broadcasted_iota(I32, (SUBLANES, LANES), 0)
    cur = [vs[_BUTTERFLY_ORDER[p]] for p in range(SUBLANES)]
    for shift, keep in ((1, sub % 2 == 0), (2, sub % 4 < 2), (4, sub < 4)):
        nxt = []
        for a, b in zip(cur[0::2], cur[1::2]):
            nxt.append(jnp.where(keep, a, b) + pltpu.roll(jnp.where(keep, b, a), shift, 0))
        cur = nxt
    return cur[0]


def _unpack_row(tab_ref, p, sh):
    w = tab_ref[p]
    return lax.bitcast_convert_type((w << sh.astype(jnp.uint32)) & HI_MASK, F32)


def _smem_tile_copy(hbm_ref, smem_ref, sem_ref, tile, slot):
    return pltpu.make_async_copy(hbm_ref.at[tile], smem_ref.at[slot], sem_ref.at[slot])


def _peer_score_kernel(pidx_hbm, shift_hbm, x_ref, gate_ref, tab_ref, act_ref, pidx_sm, shift_sm, dots_ref, sems,
                       *, n_tiles, slots):
    i = pl.program_id(0)
    slot = i % 2
    tt = PEER_TILE

    def copies(tile, sl):
        return (_smem_tile_copy(pidx_hbm, pidx_sm, sems.at[0], tile, sl),
                _smem_tile_copy(shift_hbm, shift_sm, sems.at[1], tile, sl))

    @pl.when(i == 0)
    def _():
        for c in copies(0, 0):
            c.start()

    @pl.when(i + 1 < n_tiles)
    def _():
        for c in copies(i + 1, 1 - slot):
            c.start()

    for c in copies(i, slot):
        c.wait()

    lane = lax.broadcasted_iota(I32, (slots, tt), 1)
    dots_ref[...] = jnp.zeros_like(dots_ref)

    def token(t, carry):
        x = x_ref[t]
        groups = []
        for g in range(slots // SUBLANES):
            prods = []
            for r in range(SUBLANES):
                j = (g * SUBLANES + r) * tt + t
                prods.append(_unpack_row(tab_ref, pidx_sm[slot, j], shift_sm[slot, j]) * x)
            groups.append(_sublane_sums(prods))
        col = jnp.sum(jnp.concatenate(groups, axis=0), axis=-1, keepdims=True)
        dots_ref[...] = jnp.where(lane == t, col, dots_ref[...])
        return carry

    lax.fori_loop(0, tt, token, 0)
    act_ref[0] = _gelu_erf(dots_ref[...]) * gate_ref[0]


def _peer_scores(x3, pidx, shift, gate, tab):
    nt, slots, tt = gate.shape
    flat = lambda a: a.reshape(nt, slots * tt)
    kern = functools.partial(_peer_score_kernel, n_tiles=nt, slots=slots)
    return pl.pallas_call(
        kern,
        grid=(nt,),
        in_specs=[pl.BlockSpec(memory_space=pl.ANY), pl.BlockSpec(memory_space=pl.ANY),
                  pl.BlockSpec((tt, SUBLANES, LANES), lambda i: (i, 0, 0)),
                  pl.BlockSpec((1, slots, tt), lambda i: (i, 0, 0)),
                  pl.BlockSpec(memory_space=pltpu.VMEM)],
        out_specs=pl.BlockSpec((1, slots, tt), lambda i: (i, 0, 0)),
        out_shape=jax.ShapeDtypeStruct((nt, slots, tt), F32),
        scratch_shapes=[pltpu.SMEM((2, slots * tt), I32), pltpu.SMEM((2, slots * tt), I32),
                        pltpu.VMEM((slots, tt), F32), pltpu.SemaphoreType.DMA((2, 2))],
        compiler_params=_cparams(("arbitrary",)),
        name="peer_scores",
    )(flat(pidx), flat(shift), x3, gate, tab)


def _peer_mix_kernel(pidx_hbm, shift_hbm, act_hbm, tab_ref, o_ref, pidx_sm, shift_sm, act_sm, sems, *, n_tiles, slots):
    i = pl.program_id(0)
    slot = i % 2
    tt = PEER_TILE

    def copies(tile, sl):
        return (_smem_tile_copy(pidx_hbm, pidx_sm, sems.at[0], tile, sl),
                _smem_tile_copy(shift_hbm, shift_sm, sems.at[1], tile, sl),
                _smem_tile_copy(act_hbm, act_sm, sems.at[2], tile, sl))

    @pl.when(i == 0)
    def _():
        for c in copies(0, 0):
            c.start()

    @pl.when(i + 1 < n_tiles)
    def _():
        for c in copies(i + 1, 1 - slot):
            c.start()

    for c in copies(i, slot):
        c.wait()

    def token(t, carry):
        acc = jnp.zeros((SUBLANES, LANES), F32)
        for r in range(slots):
            j = r * tt + t
            acc = acc + _unpack_row(tab_ref, pidx_sm[slot, j], shift_sm[slot, j]) * act_sm[slot, j]
        o_ref[t] = acc
        return carry

    lax.fori_loop(0, tt, token, 0)


def _peer_mix(pidx, shift, act, tab):
    nt, slots, tt = act.shape
    flat = lambda a: a.reshape(nt, slots * tt)
    kern = functools.partial(_peer_mix_kernel, n_tiles=nt, slots=slots)
    return pl.pallas_call(
        kern,
        grid=(nt,),
        in_specs=[pl.BlockSpec(memory_space=pl.ANY), pl.BlockSpec(memory_space=pl.ANY),
                  pl.BlockSpec(memory_space=pl.ANY), pl.BlockSpec(memory_space=pltpu.VMEM)],
        out_specs=pl.BlockSpec((tt, SUBLANES, LANES), lambda i: (i, 0, 0)),
        out_shape=jax.ShapeDtypeStruct((nt * tt, SUBLANES, LANES), F32),
        scratch_shapes=[pltpu.SMEM((2, slots * tt), I32), pltpu.SMEM((2, slots * tt), I32),
                        pltpu.SMEM((2, slots * tt), F32), pltpu.SemaphoreType.DMA((3, 2))],
        compiler_params=_cparams(("arbitrary",)),
        name="peer_mix",
    )(flat(pidx), flat(shift), flat(act), tab)


def _pack_table(tab):
    n, d = tab.shape
    assert n % 2 == 0 and d == SUBLANES * LANES
    bits = lax.bitcast_convert_type(tab.astype(BF16), jnp.uint16).astype(jnp.uint32).reshape(n // 2, 2, d)
    return (bits[:, 0] | (bits[:, 1] << 16)).reshape(n // 2, SUBLANES, LANES)


def _peer(hn, pq, k1ext, k2ext, u_pack, v_pack):
    t, d = hn.shape
    pidx, shift, gate = _peer_route(pq, k1ext, k2ext)
    act = _peer_scores(hn.reshape(t, SUBLANES, LANES), pidx, shift, gate, u_pack)
    return _peer_mix(pidx, shift, act, v_pack).reshape(t, d)


def _ple_kernel(x1_ref, po_ref, pe_ref, g3_ref, wg_ref, wp_ref, gf_ref, y_ref, *, final):
    x2 = x1_ref[...] + po_ref[...]
    gate = _sigmoid(_dot(_rms(x2, g3_ref[...]), wg_ref[...]))
    x3 = x2 + gate * _dot(pe_ref[...], wp_ref[...])
    y_ref[...] = _rms(x3, gf_ref[...]) if final else x3


def _ple(x1, po, pe, g3, wg, wp, gf, tm, final):
    t, d = x1.shape
    assert t % tm == 0
    row = lambda a: pl.BlockSpec((tm, a.shape[1]), lambda i: (i, 0))
    full = lambda a: pl.BlockSpec(a.shape, lambda i: (0,) * a.ndim)
    return pl.pallas_call(
        functools.partial(_ple_kernel, final=final),
        grid=(t // tm,),
        in_specs=[row(x1), row(po), row(pe), full(g3), full(wg), full(wp), full(gf)],
        out_specs=pl.BlockSpec((tm, d), lambda i: (i, 0)),
        out_shape=jax.ShapeDtypeStruct((t, d), F32),
        compiler_params=_cparams(("parallel",)),
        name="ple_final",
    )(x1, po, pe, g3, wg, wp, gf)


def kernel(x_prompt, x_sample, cache_k, cache_v, state_conv, page_table, p_prompt, p_sample, norm1_g, w_in, dw_w, dw_b, cln_g, cln_b, w_att_out, w_conv_out, w_o, norm2_g, peer_wq, peer_k1, peer_k2, peer_u, peer_v, norm3_g, w_ple, w_ple_gate, final_g):
    depth = w_in.shape[0]
    b, s, d = x_prompt.shape
    db, ds, _ = x_sample.shape
    n_phys, page = cache_k.shape[1], cache_k.shape[2]
    n_pages = page_table.shape[1]
    att_w = N_HEADS * HEAD_DIM
    assert ds == 1 and (n_pages * page) % MOBA_BLOCK == 0
    row = lambda a: a.reshape(1, -1)
    pt_flat = page_table.reshape(-1).astype(I32)
    gf = row(final_g)
    half = peer_k1.shape[-1]
    assert 2 * half == LANES

    xp = x_prompt.reshape(b * s, d)
    xs = x_sample.reshape(db, d)
    outs = [[] for _ in range(6)]
    for li in range(depth):
        bf = lambda a: a[li].astype(BF16)
        w_in_b, wao, wco, wo, wq, wpg, wpl = (bf(a) for a in (w_in, w_att_out, w_conv_out, w_o, peer_wq, w_ple_gate, w_ple))
        zeros = jnp.zeros_like(peer_k1[li])
        k1ext = jnp.concatenate([peer_k1[li], zeros], axis=1)
        k2ext = jnp.concatenate([zeros, peer_k2[li]], axis=1)
        u_pack = _pack_table(peer_u[li])
        v_pack = _pack_table(peer_v[li])
        conv_w = (dw_w[li], row(dw_b[li]), row(cln_g[li]), row(cln_b[li]))
        g1, g2, g3 = row(norm1_g[li]), row(norm2_g[li]), row(norm3_g[li])
        ck3 = cache_k[li].reshape(n_phys, page, att_w)
        cv3 = cache_v[li].reshape(n_phys, page, att_w)

        q, k, v, z, sga, sgb = _inproj(xp, g1, w_in_b, 256)
        att = _moba_prompt(q.reshape(b, s, att_w), k.reshape(b, s, att_w), v.reshape(b, s, att_w))
        z3 = z.reshape(b, s, -1)
        conv = _conv_prompt(z3, *conv_w)
        x1, hn, pq = _outproj(xp, att.reshape(b * s, att_w), conv.reshape(b * s, -1), sga, sgb, wao, wco, wo, g2, wq, 256)
        po = _peer(hn, pq, k1ext, k2ext, u_pack, v_pack)
        xp = _ple(x1, po, p_prompt[li].reshape(b * s, -1), g3, wpg, wpl, gf, 256, li == depth - 1)
        outs[0].append(k.reshape(b, s, N_HEADS, HEAD_DIM))
        outs[1].append(v.reshape(b, s, N_HEADS, HEAD_DIM))
        outs[2].append(z3[:, s - (dw_w.shape[1] - 1):, :])

        qs, ks, vs, zs, sgas, sgbs = _inproj(xs, g1, w_in_b, db)
        scores, ksum = _sample_scores(qs, ck3, pt_flat, n_pages)
        atts = _sample_attend(qs, ks, vs, scores, ksum, cv3, pt_flat, n_pages)
        convs, new_state = _conv_sample(state_conv[li], zs, *conv_w)
        x1s, hns, pqs = _outproj(xs, atts, convs, sgas, sgbs, wao, wco, wo, g2, wq, db)
        pos = _peer(hns, pqs, k1ext, k2ext, u_pack, v_pack)
        xs = _ple(x1s, pos, p_sample[li].reshape(db, -1), g3, wpg, wpl, gf, db, li == depth - 1)
        outs[3].append(ks.reshape(db, ds, N_HEADS, HEAD_DIM))
        outs[4].append(vs.reshape(db, ds, N_HEADS, HEAD_DIM))
        outs[5].append(new_state)

    return (xp.reshape(b, s, d), xs.reshape(db, ds, d)) + tuple(jnp.stack(o) for o in outs)
```

```python
import functools

import numpy as np
import jax
import jax.numpy as jnp
from jax import lax
from jax.experimental import pallas as pl
from jax.experimental.pallas import tpu as pltpu

F32 = jnp.float32
BF16 = jnp.bfloat16
I32 = jnp.int32
U32 = jnp.uint32

N_HEADS = 8
HEAD_DIM = 64
MOBA_BLOCK = 256
MOBA_TOPK = 3
PEER_HEADS = 8
PEER_TOPK = 16
EPS = 1e-6
NEG = -1e30
ATT_SCALE = HEAD_DIM ** -0.5

LANES = 128
SUBLANES = 8
VMEM_LIMIT_BYTES = 56 * 1024 * 1024

HIGHEST = lax.Precision.HIGHEST


def _cparams(sem):
    return pltpu.CompilerParams(dimension_semantics=sem, vmem_limit_bytes=VMEM_LIMIT_BYTES)


def _rms(x, g):
    return x * lax.rsqrt(jnp.mean(x * x, axis=-1, keepdims=True) + EPS) * g


def _sigmoid(x):
    return 1.0 / (1.0 + jnp.exp(-x))


def _dot(a, b):
    return jnp.dot(a.astype(BF16), b.astype(BF16), preferred_element_type=F32)


def _dot_nt(a, b, precision=None):
    return lax.dot_general(a, b, (((1,), (1,)), ((), ())), preferred_element_type=F32, precision=precision)


def _gelu_erf(x):
    return 0.5 * x * (1.0 + lax.erf(x * np.float32(1.0 / np.sqrt(2.0))))


def _inproj_kernel(x_ref, g_ref, w_ref, q_ref, k_ref, v_ref, z_ref, sga_ref, sgb_ref, *, att_w, conv_ch, d_model):
    h = _rms(x_ref[...], g_ref[...]).astype(BF16)

    def proj(lo, width):
        return jnp.dot(h, w_ref[:, lo:lo + width], preferred_element_type=F32)

    q_ref[...] = proj(0, att_w)
    k_ref[...] = proj(att_w, att_w)
    v_ref[...] = proj(2 * att_w, att_w)
    base = 3 * att_w
    z_ref[...] = proj(base, conv_ch) * _sigmoid(proj(base + conv_ch, conv_ch))
    base += 2 * conv_ch
    sga_ref[...] = _sigmoid(proj(base, d_model))
    sgb_ref[...] = _sigmoid(proj(base + d_model, d_model))


def _inproj(x2, g, w_bf, tm):
    t, d = x2.shape
    att_w = N_HEADS * HEAD_DIM
    conv_ch = d // 2
    assert w_bf.shape[1] == 3 * att_w + 2 * conv_ch + 2 * d and t % tm == 0
    row = lambda w: pl.BlockSpec((tm, w), lambda i: (i, 0))
    full = lambda a: pl.BlockSpec(a.shape, lambda i: (0,) * a.ndim)
    out_w = (att_w, att_w, att_w, conv_ch, d, d)
    return pl.pallas_call(
        functools.partial(_inproj_kernel, att_w=att_w, conv_ch=conv_ch, d_model=d),
        grid=(t // tm,),
        in_specs=[row(d), full(g), full(w_bf)],
        out_specs=[row(w) for w in out_w],
        out_shape=[jax.ShapeDtypeStruct((t, w), F32) for w in out_w],
        compiler_params=_cparams(("parallel",)),
        name="inproj",
    )(x2, g, w_bf)


def _moba_prompt_kernel(q_ref, k_ref, v_ref, o_ref, kmean_ref, sel_ref, *, nb):
    blk = MOBA_BLOCK
    i = pl.program_id(2)

    @pl.when(i == 0)
    def _():
        for j in range(nb):
            kmean_ref[j:j + 1, :] = jnp.sum(k_ref[0, j * blk:(j + 1) * blk, :], axis=0, keepdims=True) * (1.0 / blk)

    lane = lax.broadcasted_iota(I32, (1, LANES), 1)
    row_i = lax.broadcasted_iota(I32, (blk, blk), 0)
    col_i = lax.broadcasted_iota(I32, (blk, blk), 1)
    causal = col_i <= row_i
    kmean = kmean_ref[...]
    bidx = lax.broadcasted_iota(I32, (nb, blk), 0)
    expand = (lax.broadcasted_iota(I32, (LANES, nb * LANES), 1) // LANES
              == lax.broadcasted_iota(I32, (LANES, nb * LANES), 0)).astype(BF16)
    qbs = []
    for hs in range(2):
        hm = (lane < HEAD_DIM) if hs == 0 else (lane >= HEAD_DIM)
        q = jnp.where(hm, q_ref[0], 0.0)
        gates = jnp.where(bidx < i, _dot_nt(kmean, q, HIGHEST), NEG)
        rank = jnp.zeros((nb, blk), F32)
        for m in range(nb):
            gm = gates[m:m + 1, :]
            ahead = ((bidx > m) & (gm >= gates)) | ((bidx < m) & (gm > gates))
            rank = rank + ahead.astype(F32)
        sel_t = jnp.where((rank < MOBA_TOPK) & (bidx < i), 1.0, 0.0)
        sel_q = jnp.concatenate([sel_t, jnp.zeros((LANES - nb, blk), F32)], axis=0).T
        sel_all = _dot(sel_q, expand)
        for j in range(nb):
            sel_ref[hs, j] = sel_all[:, j * LANES:(j + 1) * LANES]
        qbs.append(q.astype(BF16))

    def step(s, mask, vb, carry):
        m_run, l_run, acc = carry
        s = jnp.where(mask, s * ATT_SCALE, NEG)
        m_new = jnp.maximum(m_run, jnp.max(s, axis=-1, keepdims=True))
        p = jnp.where(mask, jnp.exp(s - m_new), 0.0)
        alpha = jnp.exp(m_run - m_new)
        l_new = alpha * l_run + jnp.sum(p, axis=-1, keepdims=True)
        acc_new = alpha * acc + jnp.dot(p.astype(BF16), vb, preferred_element_type=F32)
        return m_new, l_new, acc_new

    def both_heads(j, masks, carries):
        start = pl.multiple_of(j * blk, blk)
        kb = k_ref[0, pl.ds(start, blk), :].astype(BF16)
        vb = v_ref[0, pl.ds(start, blk), :].astype(BF16)
        return tuple(step(_dot_nt(qbs[hs], kb), masks[hs], vb, carries[hs]) for hs in range(2))

    def past_body(j, carries):
        masks = []
        for hs in range(2):
            sj = sel_ref[hs, j]
            masks.append(jnp.concatenate([sj, sj], axis=1) > 0.5)
        return both_heads(j, masks, carries)

    init = (jnp.full((blk, 1), NEG, F32), jnp.zeros((blk, 1), F32), jnp.zeros((blk, LANES), F32))
    carries = lax.fori_loop(0, i, past_body, (init, init))
    (_, l_a, acc_a), (_, l_b, acc_b) = both_heads(i, (causal, causal), carries)
    o_ref[0] = jnp.where(lane < HEAD_DIM, acc_a / l_a, acc_b / l_b)


def _moba_prompt(q, k, v):
    b, s, w = q.shape
    blk = MOBA_BLOCK
    nb = s // blk
    assert s % blk == 0 and nb >= MOBA_TOPK and w % LANES == 0 and LANES == 2 * HEAD_DIM
    kv_spec = pl.BlockSpec((1, s, LANES), lambda bi, hp, i: (bi, 0, hp))
    q_spec = pl.BlockSpec((1, blk, LANES), lambda bi, hp, i: (bi, i, hp))
    return pl.pallas_call(
        functools.partial(_moba_prompt_kernel, nb=nb),
        grid=(b, w // LANES, nb),
        in_specs=[q_spec, kv_spec, kv_spec],
        out_specs=q_spec,
        out_shape=jax.ShapeDtypeStruct((b, s, w), F32),
        scratch_shapes=[pltpu.VMEM((nb, LANES), F32), pltpu.VMEM((2, nb, blk, LANES), F32)],
        compiler_params=_cparams(("parallel", "parallel", "arbitrary")),
        name="moba_prompt",
    )(q, k, v)


PAGES_PER_STEP = 8


def _sublane_sums(vs):
    sub = lax.broadcasted_iota(I32, (SUBLANES, LANES), 0)
    cur = [vs[_BUTTERFLY_ORDER[p]] for p in range(SUBLANES)]
    for shift, keep in ((1, sub % 2 == 0), (2, sub % 4 < 2), (4, sub < 4)):
        nxt = []
        for a, b in zip(cur[0::2], cur[1::2]):
            nxt.append(jnp.where(keep, a, b) + pltpu.roll(jnp.where(keep, b, a), shift, 0))
        cur = nxt
    return cur[0]


def _butterfly_order():
    ids = [np.full((SUBLANES,), i) for i in range(SUBLANES)]
    sub = np.arange(SUBLANES)
    for shift, keep in ((1, sub % 2 == 0), (2, sub % 4 < 2), (4, sub < 4)):
        nxt = []
        for a, b in zip(ids[0::2], ids[1::2]):
            first = np.where(keep, a, b)
            assert (first == np.roll(np.where(keep, b, a), shift)).all()
            nxt.append(first)
        ids = nxt
    order = np.empty(SUBLANES, np.int64)
    order[ids[0]] = np.arange(SUBLANES)
    return [int(v) for v in order]


_BUTTERFLY_ORDER = _butterfly_order()


def _head_rows(row):
    w = row.shape[-1]
    lane_head = lax.broadcasted_iota(I32, (N_HEADS, w), 1) // HEAD_DIM
    sub = lax.broadcasted_iota(I32, (N_HEADS, w), 0)
    return jnp.where(lane_head == sub, jnp.broadcast_to(row, (N_HEADS, w)), 0.0)


def _sample_scores_kernel(pt_ref, qcol_ref, *refs, page, pages_per_block):
    del pt_ref
    k_refs = refs[:PAGES_PER_STEP]
    s_ref, gsum_ref, qb_ref, gacc_ref = refs[PAGES_PER_STEP:]
    st = pl.program_id(1)
    bps = PAGES_PER_STEP // pages_per_block

    @pl.when(st == 0)
    def _():
        qb_ref[...] = jnp.broadcast_to(qcol_ref[0], qb_ref.shape)
        gacc_ref[...] = jnp.zeros_like(gacc_ref)

    lane = lax.broadcasted_iota(I32, (N_HEADS, LANES), 1)
    raws = []
    for p in range(PAGES_PER_STEP):
        per_head = []
        for h in range(N_HEADS):
            prod = k_refs[p][0, h] * qb_ref[h]
            part = prod[0:SUBLANES]
            for c in range(1, HEAD_DIM // SUBLANES):
                part = part + prod[c * SUBLANES:(c + 1) * SUBLANES]
            per_head.append(part)
        raw = _sublane_sums(per_head)
        raws.append(raw)
        s_ref[0, :, p * page:(p + 1) * page] = raw * ATT_SCALE
    gacc = gacc_ref[...]
    for bi in range(bps):
        tot = raws[bi * pages_per_block]
        for r in range(1, pages_per_block):
            tot = tot + raws[bi * pages_per_block + r]
        gacc = jnp.where(lane == st * bps + bi, jnp.sum(tot, axis=-1, keepdims=True), gacc)
    gacc_ref[...] = gacc
    gsum_ref[0] = gacc


def _page_specs(n_pages):
    def make(p):
        return pl.BlockSpec((1, N_HEADS, HEAD_DIM, LANES),
                            lambda b, s, pt: (pt[b * n_pages + s * PAGES_PER_STEP + p], 0, 0, 0))
    return [make(p) for p in range(PAGES_PER_STEP)]


def _sample_scores(qcol, cache_kt, pt_flat, n_pages):
    db = qcol.shape[0]
    page = cache_kt.shape[-1]
    ppb = MOBA_BLOCK // page
    steps = n_pages // PAGES_PER_STEP
    nb = n_pages // ppb
    assert page == LANES and n_pages % PAGES_PER_STEP == 0 and PAGES_PER_STEP % ppb == 0 and nb <= LANES
    grid_spec = pltpu.PrefetchScalarGridSpec(
        num_scalar_prefetch=1,
        grid=(db, steps),
        in_specs=[pl.BlockSpec((1, N_HEADS, HEAD_DIM, 1), lambda b, s, pt: (b, 0, 0, 0))] + _page_specs(n_pages),
        out_specs=[pl.BlockSpec((1, N_HEADS, PAGES_PER_STEP * page), lambda b, s, pt: (b, 0, s)),
                   pl.BlockSpec((1, N_HEADS, LANES), lambda b, s, pt: (b, 0, 0))],
        scratch_shapes=[pltpu.VMEM((N_HEADS, HEAD_DIM, page), F32), pltpu.VMEM((N_HEADS, LANES), F32)],
    )
    return pl.pallas_call(
        functools.partial(_sample_scores_kernel, page=page, pages_per_block=ppb),
        grid_spec=grid_spec,
        out_shape=[jax.ShapeDtypeStruct((db, N_HEADS, n_pages * page), F32),
                   jax.ShapeDtypeStruct((db, N_HEADS, LANES), F32)],
        compiler_params=_cparams(("parallel", "arbitrary")),
        name="sample_scores",
    )(pt_flat, qcol, *([cache_kt] * PAGES_PER_STEP))


def _sample_attend_kernel(pt_ref, q_ref, kn_ref, vncol_ref, s_ref, gsum_ref, *refs, page, nb, steps):
    del pt_ref
    v_refs = refs[:PAGES_PER_STEP]
    o_ref = refs[PAGES_PER_STEP]
    p_ref, acc_ref, l_ref, pn_ref = refs[PAGES_PER_STEP + 1:]
    st = pl.program_id(1)
    span = PAGES_PER_STEP * page

    @pl.when(st == 0)
    def _():
        qbd = _head_rows(q_ref[0])
        bidx = lax.broadcasted_iota(I32, (N_HEADS, LANES), 1)
        gate = jnp.where(bidx < nb, gsum_ref[0] * (1.0 / MOBA_BLOCK), -jnp.inf)
        sel = jnp.zeros((N_HEADS, LANES), F32)
        for _ in range(MOBA_TOPK):
            mx = jnp.max(gate, axis=-1, keepdims=True)
            pick = jnp.min(jnp.where(gate == mx, bidx, LANES), axis=-1, keepdims=True)
            hit = bidx == pick
            sel = jnp.where(hit, 1.0, sel)
            gate = jnp.where(hit, -jnp.inf, gate)
        n_keys = nb * MOBA_BLOCK
        expand = (lax.broadcasted_iota(I32, (LANES, n_keys), 1) // MOBA_BLOCK
                  == lax.broadcasted_iota(I32, (LANES, n_keys), 0))
        mask = _dot(sel, expand.astype(F32)) > 0.5
        s_all = s_ref[0]
        s_new = jnp.sum(qbd * kn_ref[0], axis=-1, keepdims=True) * ATT_SCALE
        m = jnp.maximum(jnp.max(jnp.where(mask, s_all, NEG), axis=-1, keepdims=True), s_new)
        p = jnp.where(mask, jnp.exp(s_all - m), 0.0)
        pn = jnp.exp(s_new - m)
        l_ref[...] = jnp.broadcast_to(jnp.sum(p, axis=-1, keepdims=True) + pn, l_ref.shape)
        pn_ref[...] = jnp.broadcast_to(pn, pn_ref.shape)
        for t in range(steps):
            p_ref[t] = p[:, t * span:(t + 1) * span]
        acc_ref[...] = jnp.zeros_like(acc_ref)

    p_step = p_ref[st]
    for h in range(N_HEADS):
        acc = acc_ref[h]
        for pg in range(PAGES_PER_STEP):
            acc = acc + v_refs[pg][0, h] * p_step[h:h + 1, pg * page:(pg + 1) * page]
        acc_ref[h] = acc

    @pl.when(st == steps - 1)
    def _():
        for h in range(N_HEADS):
            tot = jnp.sum(acc_ref[h], axis=-1, keepdims=True)
            o_ref[0, h] = (tot + pn_ref[h:h + 1, 0:1] * vncol_ref[0, h]) / l_ref[h:h + 1, 0:1]


def _sample_attend(q, k_new, vncol, scores, gsum, cache_vt, pt_flat, n_pages):
    db, w = q.shape
    page = cache_vt.shape[-1]
    steps = n_pages // PAGES_PER_STEP
    nb = n_pages * page // MOBA_BLOCK
    assert nb >= MOBA_TOPK
    one = lambda: pl.BlockSpec((1, 1, w), lambda b, s, pt: (b, 0, 0))
    col = lambda: pl.BlockSpec((1, N_HEADS, HEAD_DIM, 1), lambda b, s, pt: (b, 0, 0, 0))
    grid_spec = pltpu.PrefetchScalarGridSpec(
        num_scalar_prefetch=1,
        grid=(db, steps),
        in_specs=[one(), one(), col(),
                  pl.BlockSpec((1, N_HEADS, n_pages * page), lambda b, s, pt: (b, 0, 0)),
                  pl.BlockSpec((1, N_HEADS, LANES), lambda b, s, pt: (b, 0, 0))] + _page_specs(n_pages),
        out_specs=col(),
        scratch_shapes=[pltpu.VMEM((steps, N_HEADS, PAGES_PER_STEP * page), F32),
                        pltpu.VMEM((N_HEADS, HEAD_DIM, page), F32),
                        pltpu.VMEM((N_HEADS, LANES), F32),
                        pltpu.VMEM((N_HEADS, LANES), F32)],
    )
    r3 = lambda a: a.reshape(db, 1, w)
    out = pl.pallas_call(
        functools.partial(_sample_attend_kernel, page=page, nb=nb, steps=steps),
        grid_spec=grid_spec,
        out_shape=jax.ShapeDtypeStruct((db, N_HEADS, HEAD_DIM, 1), F32),
        compiler_params=_cparams(("parallel", "arbitrary")),
        name="sample_attend",
    )(pt_flat, r3(q), r3(k_new), vncol, scores, gsum, *([cache_vt] * PAGES_PER_STEP))
    return out.reshape(db, w)


CONV_HALO = 32


def _ln_silu(y, g, b):
    mu = jnp.mean(y, axis=-1, keepdims=True)
    var = jnp.mean(jnp.square(y - mu), axis=-1, keepdims=True)
    n = (y - mu) * lax.rsqrt(var + EPS) * g + b
    return n * _sigmoid(n)


def _conv_prompt_kernel(zc_ref, zp_ref, dw_ref, db_ref, g_ref, b_ref, o_ref, win_ref, *, cw, tile):
    i = pl.program_id(1)
    prev = zp_ref[0, tile - CONV_HALO:tile, :]
    win_ref[0:CONV_HALO, :] = jnp.where(i > 0, prev, 0.0)
    win_ref[CONV_HALO:CONV_HALO + tile, :] = zc_ref[0]
    off = CONV_HALO - (cw - 1)
    y = jnp.broadcast_to(db_ref[...], (tile, db_ref.shape[-1]))
    for w in range(cw):
        y = y + win_ref[off + w:off + w + tile, :] * dw_ref[w:w + 1, :]
    o_ref[0] = _ln_silu(y, g_ref[...], b_ref[...])


def _conv_prompt(z, dw_w, dw_b, ln_g, ln_b, tile=256):
    b, s, c = z.shape
    cw = dw_w.shape[0]
    assert s % tile == 0 and cw - 1 <= CONV_HALO <= tile
    full = lambda a: pl.BlockSpec(a.shape, lambda bi, i: (0,) * a.ndim)
    return pl.pallas_call(
        functools.partial(_conv_prompt_kernel, cw=cw, tile=tile),
        grid=(b, s // tile),
        in_specs=[pl.BlockSpec((1, tile, c), lambda bi, i: (bi, i, 0)),
                  pl.BlockSpec((1, tile, c), lambda bi, i: (bi, jnp.maximum(i - 1, 0), 0)),
                  full(dw_w), full(dw_b), full(ln_g), full(ln_b)],
        out_specs=pl.BlockSpec((1, tile, c), lambda bi, i: (bi, i, 0)),
        out_shape=jax.ShapeDtypeStruct((b, s, c), F32),
        scratch_shapes=[pltpu.VMEM((CONV_HALO + tile, c), F32)],
        compiler_params=_cparams(("parallel", "arbitrary")),
        name="conv_prompt",
    )(z, z, dw_w, dw_b, ln_g, ln_b)


SAMPLE_CONV_ROWS = 8


def _conv_sample_kernel(st_ref, z_ref, dw_ref, db_ref, g_ref, b_ref, o_ref, ns_ref, *, cw):
    rows = []
    for r in range(SAMPLE_CONV_ROWS):
        y = jnp.sum(st_ref[r] * dw_ref[0:cw - 1, :], axis=0, keepdims=True)
        rows.append(y + z_ref[r:r + 1, :] * dw_ref[cw - 1:cw, :] + db_ref[...])
        ns_ref[r, 0:cw - 2, :] = st_ref[r, 1:cw - 1, :]
        ns_ref[r, cw - 2:cw - 1, :] = z_ref[r:r + 1, :]
    o_ref[...] = _ln_silu(jnp.concatenate(rows, axis=0), g_ref[...], b_ref[...])


def _conv_sample(state, z, dw_w, dw_b, ln_g, ln_b):
    db, sw, c = state.shape
    cw = dw_w.shape[0]
    r = SAMPLE_CONV_ROWS
    assert sw == cw - 1 and db % r == 0
    full = lambda a: pl.BlockSpec(a.shape, lambda i: (0,) * a.ndim)
    return pl.pallas_call(
        functools.partial(_conv_sample_kernel, cw=cw),
        grid=(db // r,),
        in_specs=[pl.BlockSpec((r, sw, c), lambda i: (i, 0, 0)), pl.BlockSpec((r, c), lambda i: (i, 0)),
                  full(dw_w), full(dw_b), full(ln_g), full(ln_b)],
        out_specs=[pl.BlockSpec((r, c), lambda i: (i, 0)), pl.BlockSpec((r, sw, c), lambda i: (i, 0, 0))],
        out_shape=[jax.ShapeDtypeStruct((db, c), F32), jax.ShapeDtypeStruct((db, sw, c), F32)],
        compiler_params=_cparams(("parallel",)),
        name="conv_sample",
    )(state, z, dw_w, dw_b, ln_g, ln_b)


def _outproj_kernel(x_ref, att_ref, conv_ref, sga_ref, sgb_ref, wao_ref, wco_ref, wo_ref, g2_ref, wq_ref,
                    x1_ref, hn_ref, pq_ref):
    merged = sga_ref[...] * _dot(att_ref[...], wao_ref[...]) + sgb_ref[...] * _dot(conv_ref[...], wco_ref[...])
    x1 = x_ref[...] + _dot(merged, wo_ref[...])
    x1_ref[...] = x1
    hn = _rms(x1, g2_ref[...])
    hn_ref[...] = hn
    pq_ref[...] = _dot(hn, wq_ref[...])


def _outproj(x2, att, conv, sga, sgb, wao, wco, wo, g2, wq, tm):
    t, d = x2.shape
    assert t % tm == 0
    row = lambda a: pl.BlockSpec((tm, a.shape[1]), lambda i: (i, 0))
    full = lambda a: pl.BlockSpec(a.shape, lambda i: (0,) * a.ndim)
    acts = (x2, att, conv, sga, sgb)
    wts = (wao, wco, wo, g2, wq)
    pq_w = wq.shape[1]
    return pl.pallas_call(
        _outproj_kernel,
        grid=(t // tm,),
        in_specs=[row(a) for a in acts] + [full(a) for a in wts],
        out_specs=[pl.BlockSpec((tm, d), lambda i: (i, 0)), pl.BlockSpec((tm, d), lambda i: (i, 0)),
                   pl.BlockSpec((tm, pq_w), lambda i: (i, 0))],
        out_shape=[jax.ShapeDtypeStruct((t, d), F32), jax.ShapeDtypeStruct((t, d), F32),
                   jax.ShapeDtypeStruct((t, pq_w), F32)],
        compiler_params=_cparams(("parallel",)),
        name="outproj",
    )(*acts, *wts)


PEER_TILE = 128
CAND_ROWS = PEER_TOPK + 8 * SUBLANES


def _peer_route_kernel(pq_ref, k1_ref, k2_ref, eidx_ref, pidx_ref, shift_ref, gate_ref, s_ref, c_ref, ci_ref, *, n_keys):
    kk = PEER_TOPK
    t = PEER_TILE
    nh = PEER_HEADS
    half = kk // 2
    assert kk == 16 and half == SUBLANES
    key_id = lax.broadcasted_iota(I32, (n_keys, t), 0)
    kidx = lax.broadcasted_iota(I32, (kk, t), 0)
    for h in range(nh):
        qh = pq_ref[:, h * LANES:(h + 1) * LANES]
        s_ref[2 * h] = _dot_nt(k1_ref[...], qh, HIGHEST)
        s_ref[2 * h + 1] = _dot_nt(k2_ref[...], qh, HIGHEST)

    def level1(it, carry):
        vals, idxs = carry
        nv, ni = [], []
        for c in range(2 * nh):
            s = s_ref[c]
            mx = jnp.max(s, axis=0, keepdims=True)
            pick = jnp.min(jnp.where(s == mx, key_id, n_keys), axis=0, keepdims=True)
            s_ref[c] = jnp.where(key_id == pick, -jnp.inf, s)
            nv.append(jnp.where(kidx == it, mx, vals[c]))
            ni.append(jnp.where(kidx == it, pick, idxs[c]))
        return tuple(nv), tuple(ni)

    zf = tuple(jnp.zeros((kk, t), F32) for _ in range(2 * nh))
    zi = tuple(jnp.zeros((kk, t), I32) for _ in range(2 * nh))
    vals, idxs = lax.fori_loop(0, kk, level1, (zf, zi))

    b16 = lax.broadcasted_iota(I32, (kk, t), 0)
    b8 = lax.broadcasted_iota(I32, (half, t), 0)
    flat_parts = [b16, kk + b8]
    for a in range(2, half):
        flat_parts.append(a * kk + b8)
    flat_parts.append((half + b8) * kk)
    flat = jnp.concatenate(flat_parts, axis=0)
    for h in range(nh):
        v1, v2, i1, i2 = vals[2 * h], vals[2 * h + 1], idxs[2 * h], idxs[2 * h + 1]
        c_parts = [v1[0:1] + v2, v1[1:2] + v2[0:half]]
        i_parts = [i1[0:1] * n_keys + i2, i1[1:2] * n_keys + i2[0:half]]
        for a in range(2, half):
            c_parts.append(jnp.where(b8 < kk // (a + 1), v1[a:a + 1] + v2[0:half], -jnp.inf))
            i_parts.append(i1[a:a + 1] * n_keys + i2[0:half])
        c_parts.append(v1[half:kk] + v2[0:1])
        i_parts.append(i1[half:kk] * n_keys + i2[0:1])
        c_ref[h] = jnp.concatenate(c_parts, axis=0)
        ci_ref[h] = jnp.concatenate(i_parts, axis=0)

    def level2(it, carry):
        vals2, exps = carry
        nv, ne = [], []
        for h in range(nh):
            c = c_ref[h]
            mx = jnp.max(c, axis=0, keepdims=True)
            pick = jnp.min(jnp.where(c == mx, flat, kk * kk), axis=0, keepdims=True)
            hit = flat == pick
            e = jnp.max(jnp.where(hit, ci_ref[h], -1), axis=0, keepdims=True)
            c_ref[h] = jnp.where(hit, -jnp.inf, c)
            nv.append(jnp.where(kidx == it, mx, vals2[h]))
            ne.append(jnp.where(kidx == it, e, exps[h]))
        return tuple(nv), tuple(ne)

    sc, eidx = lax.fori_loop(0, kk, level2, (zf[:nh], zi[:nh]))
    for h in range(nh):
        ex = jnp.exp(sc[h] - jnp.max(sc[h], axis=0, keepdims=True))
        rows = slice(h * kk, (h + 1) * kk)
        gate_ref[0, rows, :] = ex / jnp.sum(ex, axis=0, keepdims=True)
        eidx_ref[0, rows, :] = eidx[h]
        pidx_ref[0, rows, :] = eidx[h] >> 1
        shift_ref[0, rows, :] = ((eidx[h] & 1) ^ 1) << 4


def _peer_route(pq, k1ext, k2ext):
    t, w = pq.shape
    n_keys = k1ext.shape[0]
    slots = PEER_HEADS * PEER_TOPK
    assert t % PEER_TILE == 0 and w == PEER_HEADS * LANES and k1ext.shape[1] == LANES
    nt = t // PEER_TILE
    full = lambda a: pl.BlockSpec(a.shape, lambda i: (0,) * a.ndim)
    ospec = pl.BlockSpec((1, slots, PEER_TILE), lambda i: (i, 0, 0))
    ishape = jax.ShapeDtypeStruct((nt, slots, PEER_TILE), I32)
    return pl.pallas_call(
        functools.partial(_peer_route_kernel, n_keys=n_keys),
        grid=(nt,),
        in_specs=[pl.BlockSpec((PEER_TILE, w), lambda i: (i, 0)), full(k1ext), full(k2ext)],
        out_specs=[ospec, ospec, ospec, ospec],
        out_shape=[ishape, ishape, ishape, jax.ShapeDtypeStruct((nt, slots, PEER_TILE), F32)],
        scratch_shapes=[pltpu.VMEM((2 * PEER_HEADS, n_keys, PEER_TILE), F32),
                        pltpu.VMEM((PEER_HEADS, CAND_ROWS, PEER_TILE), F32),
                        pltpu.VMEM((PEER_HEADS, CAND_ROWS, PEER_TILE), I32)],
        compiler_params=_cparams(("parallel",)),
        name="peer_route",
    )(pq, k1ext, k2ext)


HI_MASK = np.uint32(0xFFFF0000)
HALF_SUB = SUBLANES // 2


def _as_f32(bits):
    return lax.bitcast_convert_type(bits, F32)


def _tile_copies(hbm_refs, smem_refs, sems, tile, slot, n):
    dst = pl.ds(pl.multiple_of(slot * n, n), n)
    return [pltpu.make_async_copy(h.at[tile], s.at[dst], sems.at[k, slot])
            for k, (h, s) in enumerate(zip(hbm_refs, smem_refs))]


def _stream_tiles(hbm_refs, smem_refs, sems, n_tiles, n):
    i = pl.program_id(0)
    slot = i % 2

    @pl.when(i == 0)
    def _():
        for c in _tile_copies(hbm_refs, smem_refs, sems, 0, 0, n):
            c.start()

    @pl.when(i + 1 < n_tiles)
    def _():
        for c in _tile_copies(hbm_refs, smem_refs, sems, i + 1, 1 - slot, n):
            c.start()

    for c in _tile_copies(hbm_refs, smem_refs, sems, i, slot, n):
        c.wait()
    return slot


SCORE_CHUNK_GROUPS = 8


def _peer_score_kernel(pidx_hbm, shift_hbm, x_ref, gate_ref, tab_ref, act_ref, pidx_sm, shift_sm, dots_ref, grp_ref, sems,
                       *, n_tiles, slots):
    tt = PEER_TILE
    n = slots * tt
    slot = _stream_tiles((pidx_hbm, shift_hbm), (pidx_sm, shift_sm), sems, n_tiles, n)
    lane = lax.broadcasted_iota(I32, (slots, tt), 1)
    dots_ref[...] = jnp.zeros_like(dots_ref)
    chunk_rows = SCORE_CHUNK_GROUPS * SUBLANES
    span = chunk_rows * tt
    view_len = 2 * n - (chunk_rows - 1) * tt

    lane_c = lax.broadcasted_iota(I32, (chunk_rows, tt), 1)
    grp_ref[...] = jnp.zeros_like(grp_ref)

    def finish(tok, buf, c):
        rows = pl.ds(pl.multiple_of(c * chunk_rows, chunk_rows), chunk_rows)
        part = grp_ref[buf, pl.ds(c * SCORE_CHUNK_GROUPS, SCORE_CHUNK_GROUPS)]
        col = jnp.sum(part.reshape(chunk_rows, LANES), axis=-1, keepdims=True)
        dots_ref[rows, :] = jnp.where(lane_c == tok, col, dots_ref[rows, :])

    def token(t, carry):
        base = slot * n + t
        x = x_ref[t]
        cur = t % 2

        def chunk(c, inner):
            finish(t - 1, 1 - cur, c)
            cbase = base + c * span
            for g in range(SCORE_CHUNK_GROUPS):
                prods = []
                for r in range(SUBLANES):
                    view = pl.ds((g * SUBLANES + r) * tt, view_len)
                    w = tab_ref[pidx_sm.at[view][cbase]]
                    prods.append(_as_f32((w << shift_sm.at[view][cbase].astype(U32)) & HI_MASK) * x)
                grp_ref[cur, c * SCORE_CHUNK_GROUPS + g] = _sublane_sums(prods)
            return inner

        lax.fori_loop(0, slots // chunk_rows, chunk, 0)
        return carry

    lax.fori_loop(0, tt, token, 0)
    for c in range(slots // chunk_rows):
        finish(tt - 1, (tt - 1) % 2, c)
    act_ref[0] = _gelu_erf(dots_ref[...]) * gate_ref[0]


def _peer_scores(x3, pidx, shift, gate, tab):
    nt, slots, tt = gate.shape
    n = slots * tt
    flat = lambda a: a.reshape(nt, n)
    kern = functools.partial(_peer_score_kernel, n_tiles=nt, slots=slots)
    return pl.pallas_call(
        kern,
        grid=(nt,),
        in_specs=[pl.BlockSpec(memory_space=pl.ANY), pl.BlockSpec(memory_space=pl.ANY),
                  pl.BlockSpec((tt, SUBLANES, LANES), lambda i: (i, 0, 0)),
                  pl.BlockSpec((1, slots, tt), lambda i: (i, 0, 0)),
                  pl.BlockSpec(memory_space=pltpu.VMEM)],
        out_specs=pl.BlockSpec((1, slots, tt), lambda i: (i, 0, 0)),
        out_shape=jax.ShapeDtypeStruct((nt, slots, tt), F32),
        scratch_shapes=[pltpu.SMEM((2 * n,), I32), pltpu.SMEM((2 * n,), I32),
                        pltpu.VMEM((slots, tt), F32), pltpu.VMEM((2, slots // SUBLANES, SUBLANES, LANES), F32),
                        pltpu.SemaphoreType.DMA((2, 2))],
        compiler_params=_cparams(("arbitrary",)),
        name="peer_scores",
    )(flat(pidx), flat(shift), x3, gate, tab)


MIX_ACCS = 4


MIX_CHUNK = 64


def _peer_mix_kernel(eidx_hbm, act_hbm, tab_ref, o_ref, eidx_sm, act_sm, sems, *, n_tiles, slots):
    tt = PEER_TILE
    n = slots * tt
    slot = _stream_tiles((eidx_hbm, act_hbm), (eidx_sm, act_sm), sems, n_tiles, n)
    span = MIX_CHUNK * tt
    view_len = 2 * n - (MIX_CHUNK - 1) * tt

    def token(t, carry):
        base = slot * n + t

        def chunk(c, accs):
            lo, hi = list(accs[0]), list(accs[1])
            cbase = base + c * span
            for r in range(MIX_CHUNK):
                view = pl.ds(r * tt, view_len)
                w = tab_ref[eidx_sm.at[view][cbase]]
                a = act_sm.at[view][cbase]
                lo[r % MIX_ACCS] = lo[r % MIX_ACCS] + a * _as_f32(w << 16)
                hi[r % MIX_ACCS] = hi[r % MIX_ACCS] + a * _as_f32(w & HI_MASK)
            return tuple(lo), tuple(hi)

        zero = tuple(jnp.zeros((HALF_SUB, LANES), F32) for _ in range(MIX_ACCS))
        lo, hi = lax.fori_loop(0, slots // MIX_CHUNK, chunk, (zero, zero))
        o_ref[t, 0:HALF_SUB, :] = (lo[0] + lo[1]) + (lo[2] + lo[3])
        o_ref[t, HALF_SUB:SUBLANES, :] = (hi[0] + hi[1]) + (hi[2] + hi[3])
        return carry

    lax.fori_loop(0, tt, token, 0)


def _peer_mix(eidx, act, tab):
    nt, slots, tt = act.shape
    n = slots * tt
    flat = lambda a: a.reshape(nt, n)
    kern = functools.partial(_peer_mix_kernel, n_tiles=nt, slots=slots)
    return pl.pallas_call(
        kern,
        grid=(nt,),
        in_specs=[pl.BlockSpec(memory_space=pl.ANY), pl.BlockSpec(memory_space=pl.ANY),
                  pl.BlockSpec(memory_space=pltpu.VMEM)],
        out_specs=pl.BlockSpec((tt, SUBLANES, LANES), lambda i: (i, 0, 0)),
        out_shape=jax.ShapeDtypeStruct((nt * tt, SUBLANES, LANES), F32),
        scratch_shapes=[pltpu.SMEM((2 * n,), I32), pltpu.SMEM((2 * n,), F32), pltpu.SemaphoreType.DMA((2, 2))],
        compiler_params=_cparams(("arbitrary",)),
        name="peer_mix",
    )(flat(eidx), flat(act), tab)


def _bf16_bits(tab):
    return lax.bitcast_convert_type(tab.astype(BF16), jnp.uint16).astype(U32)


def _pack_pairs(tab):
    n, d = tab.shape
    assert n % 2 == 0 and d == SUBLANES * LANES
    bits = _bf16_bits(tab).reshape(n // 2, 2, d)
    return (bits[:, 0] | (bits[:, 1] << 16)).reshape(n // 2, SUBLANES, LANES)


def _pack_halves(tab):
    n, d = tab.shape
    assert d == SUBLANES * LANES
    bits = _bf16_bits(tab)
    return (bits[:, :d // 2] | (bits[:, d // 2:] << 16)).reshape(n, HALF_SUB, LANES)


def _peer(hn, pq, k1ext, k2ext, u_pack, v_pack):
    t, d = hn.shape
    eidx, pidx, shift, gate = _peer_route(pq, k1ext, k2ext)
    act = _peer_scores(hn.reshape(t, SUBLANES, LANES), pidx, shift, gate, u_pack)
    return _peer_mix(eidx, act, v_pack).reshape(t, d)


def _ple_kernel(x1_ref, po_ref, pe_ref, g3_ref, wg_ref, wp_ref, gf_ref, y_ref, *, final):
    x2 = x1_ref[...] + po_ref[...]
    gate = _sigmoid(_dot(_rms(x2, g3_ref[...]), wg_ref[...]))
    x3 = x2 + gate * _dot(pe_ref[...], wp_ref[...])
    y_ref[...] = _rms(x3, gf_ref[...]) if final else x3


def _ple(x1, po, pe, g3, wg, wp, gf, tm, final):
    t, d = x1.shape
    assert t % tm == 0
    row = lambda a: pl.BlockSpec((tm, a.shape[1]), lambda i: (i, 0))
    full = lambda a: pl.BlockSpec(a.shape, lambda i: (0,) * a.ndim)
    return pl.pallas_call(
        functools.partial(_ple_kernel, final=final),
        grid=(t // tm,),
        in_specs=[row(x1), row(po), row(pe), full(g3), full(wg), full(wp), full(gf)],
        out_specs=pl.BlockSpec((tm, d), lambda i: (i, 0)),
        out_shape=jax.ShapeDtypeStruct((t, d), F32),
        compiler_params=_cparams(("parallel",)),
        name="ple_final",
    )(x1, po, pe, g3, wg, wp, gf)


def kernel(x_prompt, x_sample, cache_k, cache_v, state_conv, page_table, p_prompt, p_sample, norm1_g, w_in, dw_w, dw_b, cln_g, cln_b, w_att_out, w_conv_out, w_o, norm2_g, peer_wq, peer_k1, peer_k2, peer_u, peer_v, norm3_g, w_ple, w_ple_gate, final_g):
    depth = w_in.shape[0]
    b, s, d = x_prompt.shape
    db, ds, _ = x_sample.shape
    page = cache_k.shape[2]
    n_pages = page_table.shape[1]
    att_w = N_HEADS * HEAD_DIM
    assert ds == 1 and (n_pages * page) % MOBA_BLOCK == 0
    row = lambda a: a.reshape(1, -1)
    pt_flat = page_table.reshape(-1).astype(I32)
    gf = row(final_g)
    assert 2 * peer_k1.shape[-1] == LANES

    xp = x_prompt.reshape(b * s, d)
    xs = x_sample.reshape(db, d)
    outs = [[] for _ in range(6)]
    for li in range(depth):
        last = li == depth - 1
        bf = lambda a: a[li].astype(BF16)
        w_in_b, wao, wco, wo, wq, wpg, wpl = (bf(a) for a in (w_in, w_att_out, w_conv_out, w_o, peer_wq, w_ple_gate, w_ple))
        zeros = jnp.zeros_like(peer_k1[li])
        k1ext = jnp.concatenate([peer_k1[li], zeros], axis=1)
        k2ext = jnp.concatenate([zeros, peer_k2[li]], axis=1)
        u_pack = _pack_pairs(peer_u[li])
        v_pack = _pack_halves(peer_v[li])
        conv_w = (dw_w[li], row(dw_b[li]), row(cln_g[li]), row(cln_b[li]))
        g1, g2, g3 = row(norm1_g[li]), row(norm2_g[li]), row(norm3_g[li])
        ckt = jnp.transpose(cache_k[li], (0, 2, 3, 1))
        cvt = jnp.transpose(cache_v[li], (0, 2, 3, 1))

        qs, ks, vs, zs, sgas, sgbs = _inproj(xs, g1, w_in_b, db)
        col = lambda a: a.reshape(db, N_HEADS, HEAD_DIM, 1)
        scores, gsum = _sample_scores(col(qs), ckt, pt_flat, n_pages)
        atts = _sample_attend(qs, ks, col(vs), scores, gsum, cvt, pt_flat, n_pages)
        convs, new_state = _conv_sample(state_conv[li], zs, *conv_w)
        x1s, hns, pqs = _outproj(xs, atts, convs, sgas, sgbs, wao, wco, wo, g2, wq, db)
        pos = _peer(hns, pqs, k1ext, k2ext, u_pack, v_pack)
        xs = _ple(x1s, pos, p_sample[li].reshape(db, -1), g3, wpg, wpl, gf, db, last)
        outs[3].append(ks.reshape(db, ds, N_HEADS, HEAD_DIM))
        outs[4].append(vs.reshape(db, ds, N_HEADS, HEAD_DIM))
        outs[5].append(new_state)

        q, k, v, z, sga, sgb = _inproj(xp, g1, w_in_b, 256)
        att = _moba_prompt(q.reshape(b, s, att_w), k.reshape(b, s, att_w), v.reshape(b, s, att_w))
        z3 = z.reshape(b, s, -1)
        conv = _conv_prompt(z3, *conv_w)
        x1, hn, pq = _outproj(xp, att.reshape(b * s, att_w), conv.reshape(b * s, -1), sga, sgb, wao, wco, wo, g2, wq, 256)
        po = _peer(hn, pq, k1ext, k2ext, u_pack, v_pack)
        xp = _ple(x1, po, p_prompt[li].reshape(b * s, -1), g3, wpg, wpl, gf, 256, last)
        outs[0].append(k.reshape(b, s, N_HEADS, HEAD_DIM))
        outs[1].append(v.reshape(b, s, N_HEADS, HEAD_DIM))
        outs[2].append(z3[:, s - (dw_w.shape[1] - 1):, :])

    return (xp.reshape(b, s, d), xs.reshape(db, ds, d)) + tuple(jnp.stack(o) for o in outs)
```

```python
import functools

import numpy as np
import jax
import jax.numpy as jnp
from jax import lax
from jax.experimental import pallas as pl
from jax.experimental.pallas import tpu as pltpu

F32 = jnp.float32
BF16 = jnp.bfloat16
I32 = jnp.int32
U32 = jnp.uint32

N_HEADS = 8
HEAD_DIM = 64
MOBA_BLOCK = 256
MOBA_TOPK = 3
PEER_HEADS = 8
PEER_TOPK = 16
EPS = 1e-6
NEG = -1e30
ATT_SCALE = HEAD_DIM ** -0.5

LANES = 128
SUBLANES = 8
VMEM_LIMIT_BYTES = 56 * 1024 * 1024

HIGHEST = lax.Precision.HIGHEST


def _cparams(sem):
    return pltpu.CompilerParams(dimension_semantics=sem, vmem_limit_bytes=VMEM_LIMIT_BYTES)


def _rms(x, g):
    return x * lax.rsqrt(jnp.mean(x * x, axis=-1, keepdims=True) + EPS) * g


def _sigmoid(x):
    return 1.0 / (1.0 + jnp.exp(-x))


def _dot(a, b):
    return jnp.dot(a.astype(BF16), b.astype(BF16), preferred_element_type=F32)


def _dot_nt(a, b, precision=None):
    return lax.dot_general(a, b, (((1,), (1,)), ((), ())), preferred_element_type=F32, precision=precision)


def _gelu_erf(x):
    return 0.5 * x * (1.0 + lax.erf(x * np.float32(1.0 / np.sqrt(2.0))))


def _inproj_kernel(x_ref, g_ref, w_ref, q_ref, k_ref, v_ref, z_ref, sga_ref, sgb_ref, *, att_w, conv_ch, d_model):
    h = _rms(x_ref[...], g_ref[...]).astype(BF16)

    def proj(lo, width):
        return jnp.dot(h, w_ref[:, lo:lo + width], preferred_element_type=F32)

    q_ref[...] = proj(0, att_w)
    k_ref[...] = proj(att_w, att_w)
    v_ref[...] = proj(2 * att_w, att_w)
    base = 3 * att_w
    z_ref[...] = proj(base, conv_ch) * _sigmoid(proj(base + conv_ch, conv_ch))
    base += 2 * conv_ch
    sga_ref[...] = _sigmoid(proj(base, d_model))
    sgb_ref[...] = _sigmoid(proj(base + d_model, d_model))


def _inproj(x2, g, w_bf, tm):
    t, d = x2.shape
    att_w = N_HEADS * HEAD_DIM
    conv_ch = d // 2
    assert w_bf.shape[1] == 3 * att_w + 2 * conv_ch + 2 * d and t % tm == 0
    row = lambda w: pl.BlockSpec((tm, w), lambda i: (i, 0))
    full = lambda a: pl.BlockSpec(a.shape, lambda i: (0,) * a.ndim)
    out_w = (att_w, att_w, att_w, conv_ch, d, d)
    return pl.pallas_call(
        functools.partial(_inproj_kernel, att_w=att_w, conv_ch=conv_ch, d_model=d),
        grid=(t // tm,),
        in_specs=[row(d), full(g), full(w_bf)],
        out_specs=[row(w) for w in out_w],
        out_shape=[jax.ShapeDtypeStruct((t, w), F32) for w in out_w],
        compiler_params=_cparams(("parallel",)),
        name="inproj",
    )(x2, g, w_bf)


def _moba_prompt_kernel(q_ref, k_ref, v_ref, o_ref, kmean_ref, sel_ref, *, nb):
    blk = MOBA_BLOCK
    i = pl.program_id(2)

    @pl.when(i == 0)
    def _():
        for j in range(nb):
            kmean_ref[j:j + 1, :] = jnp.sum(k_ref[0, j * blk:(j + 1) * blk, :], axis=0, keepdims=True) * (1.0 / blk)

    lane = lax.broadcasted_iota(I32, (1, LANES), 1)
    row_i = lax.broadcasted_iota(I32, (blk, blk), 0)
    col_i = lax.broadcasted_iota(I32, (blk, blk), 1)
    causal = col_i <= row_i
    kmean = kmean_ref[...]
    bidx = lax.broadcasted_iota(I32, (nb, blk), 0)
    expand = (lax.broadcasted_iota(I32, (LANES, nb * LANES), 1) // LANES
              == lax.broadcasted_iota(I32, (LANES, nb * LANES), 0)).astype(BF16)
    qbs = []
    for hs in range(2):
        hm = (lane < HEAD_DIM) if hs == 0 else (lane >= HEAD_DIM)
        q = jnp.where(hm, q_ref[0], 0.0)
        gates = jnp.where(bidx < i, _dot_nt(kmean, q, HIGHEST), NEG)
        rank = jnp.zeros((nb, blk), F32)
        for m in range(nb):
            gm = gates[m:m + 1, :]
            ahead = ((bidx > m) & (gm >= gates)) | ((bidx < m) & (gm > gates))
            rank = rank + ahead.astype(F32)
        sel_t = jnp.where((rank < MOBA_TOPK) & (bidx < i), 1.0, 0.0)
        sel_q = jnp.concatenate([sel_t, jnp.zeros((LANES - nb, blk), F32)], axis=0).T
        sel_all = _dot(sel_q, expand)
        for j in range(nb):
            sel_ref[hs, j] = sel_all[:, j * LANES:(j + 1) * LANES]
        qbs.append(q.astype(BF16))

    def step(s, mask, vb, carry):
        m_run, l_run, acc = carry
        s = jnp.where(mask, s * ATT_SCALE, NEG)
        m_new = jnp.maximum(m_run, jnp.max(s, axis=-1, keepdims=True))
        p = jnp.where(mask, jnp.exp(s - m_new), 0.0)
        alpha = jnp.exp(m_run - m_new)
        l_new = alpha * l_run + jnp.sum(p, axis=-1, keepdims=True)
        acc_new = alpha * acc + jnp.dot(p.astype(BF16), vb, preferred_element_type=F32)
        return m_new, l_new, acc_new

    def both_heads(j, masks, carries):
        start = pl.multiple_of(j * blk, blk)
        kb = k_ref[0, pl.ds(start, blk), :].astype(BF16)
        vb = v_ref[0, pl.ds(start, blk), :].astype(BF16)
        return tuple(step(_dot_nt(qbs[hs], kb), masks[hs], vb, carries[hs]) for hs in range(2))

    def past_body(j, carries):
        masks = []
        for hs in range(2):
            sj = sel_ref[hs, j]
            masks.append(jnp.concatenate([sj, sj], axis=1) > 0.5)
        return both_heads(j, masks, carries)

    init = (jnp.full((blk, 1), NEG, F32), jnp.zeros((blk, 1), F32), jnp.zeros((blk, LANES), F32))
    carries = lax.fori_loop(0, i, past_body, (init, init))
    (_, l_a, acc_a), (_, l_b, acc_b) = both_heads(i, (causal, causal), carries)
    o_ref[0] = jnp.where(lane < HEAD_DIM, acc_a / l_a, acc_b / l_b)


def _moba_prompt(q, k, v):
    b, s, w = q.shape
    blk = MOBA_BLOCK
    nb = s // blk
    assert s % blk == 0 and nb >= MOBA_TOPK and w % LANES == 0 and LANES == 2 * HEAD_DIM
    kv_spec = pl.BlockSpec((1, s, LANES), lambda bi, hp, i: (bi, 0, hp))
    q_spec = pl.BlockSpec((1, blk, LANES), lambda bi, hp, i: (bi, i, hp))
    return pl.pallas_call(
        functools.partial(_moba_prompt_kernel, nb=nb),
        grid=(b, w // LANES, nb),
        in_specs=[q_spec, kv_spec, kv_spec],
        out_specs=q_spec,
        out_shape=jax.ShapeDtypeStruct((b, s, w), F32),
        scratch_shapes=[pltpu.VMEM((nb, LANES), F32), pltpu.VMEM((2, nb, blk, LANES), F32)],
        compiler_params=_cparams(("parallel", "parallel", "arbitrary")),
        name="moba_prompt",
    )(q, k, v)


PAGES_PER_STEP = 16


def _sublane_sums(vs):
    sub = lax.broadcasted_iota(I32, (SUBLANES, LANES), 0)
    cur = [vs[_BUTTERFLY_ORDER[p]] for p in range(SUBLANES)]
    for shift, keep in ((1, sub % 2 == 0), (2, sub % 4 < 2), (4, sub < 4)):
        nxt = []
        for a, b in zip(cur[0::2], cur[1::2]):
            nxt.append(jnp.where(keep, a, b) + pltpu.roll(jnp.where(keep, b, a), shift, 0))
        cur = nxt
    return cur[0]


def _butterfly_order():
    ids = [np.full((SUBLANES,), i) for i in range(SUBLANES)]
    sub = np.arange(SUBLANES)
    for shift, keep in ((1, sub % 2 == 0), (2, sub % 4 < 2), (4, sub < 4)):
        nxt = []
        for a, b in zip(ids[0::2], ids[1::2]):
            first = np.where(keep, a, b)
            assert (first == np.roll(np.where(keep, b, a), shift)).all()
            nxt.append(first)
        ids = nxt
    order = np.empty(SUBLANES, np.int64)
    order[ids[0]] = np.arange(SUBLANES)
    return [int(v) for v in order]


_BUTTERFLY_ORDER = _butterfly_order()


def _head_rows(row):
    w = row.shape[-1]
    lane_head = lax.broadcasted_iota(I32, (N_HEADS, w), 1) // HEAD_DIM
    sub = lax.broadcasted_iota(I32, (N_HEADS, w), 0)
    return jnp.where(lane_head == sub, jnp.broadcast_to(row, (N_HEADS, w)), 0.0)


def _sample_scores_kernel(pt_ref, qcol_ref, q_ref, kn_ref, *refs, page, pages_per_block, nb, steps):
    del pt_ref
    k_refs = refs[:PAGES_PER_STEP]
    p_ref, stat_ref, need_ref, qb_ref, gacc_ref = refs[PAGES_PER_STEP:]
    st = pl.program_id(1)
    bps = PAGES_PER_STEP // pages_per_block
    span = PAGES_PER_STEP * page

    @pl.when(st == 0)
    def _():
        qb_ref[...] = jnp.broadcast_to(qcol_ref[0], qb_ref.shape)
        gacc_ref[...] = jnp.zeros_like(gacc_ref)

    lane = lax.broadcasted_iota(I32, (N_HEADS, LANES), 1)
    raws = []
    for p in range(PAGES_PER_STEP):
        per_head = []
        for h in range(N_HEADS):
            prod = k_refs[p][0, h] * qb_ref[h]
            part = prod[0:SUBLANES]
            for c in range(1, HEAD_DIM // SUBLANES):
                part = part + prod[c * SUBLANES:(c + 1) * SUBLANES]
            per_head.append(part)
        raw = _sublane_sums(per_head)
        raws.append(raw)
        p_ref[0, st, :, p * page:(p + 1) * page] = raw * ATT_SCALE
    gacc = gacc_ref[...]
    for bi in range(bps):
        tot = raws[bi * pages_per_block]
        for r in range(1, pages_per_block):
            tot = tot + raws[bi * pages_per_block + r]
        gacc = jnp.where(lane == st * bps + bi, jnp.sum(tot, axis=-1, keepdims=True), gacc)
    gacc_ref[...] = gacc

    @pl.when(st == steps - 1)
    def _():
        gate = jnp.where(lane < nb, gacc * (1.0 / MOBA_BLOCK), -jnp.inf)
        sel = jnp.zeros((N_HEADS, LANES), F32)
        for _ in range(MOBA_TOPK):
            mx = jnp.max(gate, axis=-1, keepdims=True)
            pick = jnp.min(jnp.where(gate == mx, lane, LANES), axis=-1, keepdims=True)
            hit = lane == pick
            sel = jnp.where(hit, 1.0, sel)
            gate = jnp.where(hit, -jnp.inf, gate)
        n_keys = nb * MOBA_BLOCK
        expand = (lax.broadcasted_iota(I32, (LANES, n_keys), 1) // MOBA_BLOCK
                  == lax.broadcasted_iota(I32, (LANES, n_keys), 0))
        mask = _dot(sel, expand.astype(F32)) > 0.5
        masks = [mask[:, t * span:(t + 1) * span] for t in range(steps)]
        s_new = jnp.sum(_head_rows(q_ref[0]) * kn_ref[0], axis=-1, keepdims=True) * ATT_SCALE
        m = s_new
        for t in range(steps):
            m = jnp.maximum(m, jnp.max(jnp.where(masks[t], p_ref[0, t], NEG), axis=-1, keepdims=True))
        pn = jnp.exp(s_new - m)
        l = pn
        for t in range(steps):
            p = jnp.where(masks[t], jnp.exp(p_ref[0, t] - m), 0.0)
            l = l + jnp.sum(p, axis=-1, keepdims=True)
            p_ref[0, t] = p
        stat_ref[0] = jnp.where(lane == 0, l, jnp.where(lane == 1, pn, 0.0))
        need_ref[0] = jnp.max(sel, axis=0, keepdims=True).astype(I32)


def _page_specs(n_pages):
    def make(p):
        return pl.BlockSpec((1, N_HEADS, HEAD_DIM, LANES),
                            lambda b, s, pages, *_: (pages[b * n_pages + s * PAGES_PER_STEP + p], 0, 0, 0))
    return [make(p) for p in range(PAGES_PER_STEP)]


def _sample_scores(q, k_new, qcol, cache_kt, pt_flat, n_pages):
    db, w = q.shape
    page = cache_kt.shape[-1]
    ppb = MOBA_BLOCK // page
    steps = n_pages // PAGES_PER_STEP
    nb = n_pages // ppb
    span = PAGES_PER_STEP * page
    assert page == LANES and n_pages % PAGES_PER_STEP == 0 and PAGES_PER_STEP % ppb == 0 and MOBA_TOPK <= nb <= LANES
    one = lambda: pl.BlockSpec((1, 1, w), lambda b, s, pt: (b, 0, 0))
    grid_spec = pltpu.PrefetchScalarGridSpec(
        num_scalar_prefetch=1,
        grid=(db, steps),
        in_specs=[pl.BlockSpec((1, N_HEADS, HEAD_DIM, 1), lambda b, s, pt: (b, 0, 0, 0)), one(), one()]
        + _page_specs(n_pages),
        out_specs=[pl.BlockSpec((1, steps, N_HEADS, span), lambda b, s, pt: (b, 0, 0, 0)),
                   pl.BlockSpec((1, N_HEADS, LANES), lambda b, s, pt: (b, 0, 0)),
                   pl.BlockSpec((1, 1, LANES), lambda b, s, pt: (b, 0, 0))],
        scratch_shapes=[pltpu.VMEM((N_HEADS, HEAD_DIM, page), F32), pltpu.VMEM((N_HEADS, LANES), F32)],
    )
    r3 = lambda a: a.reshape(db, 1, w)
    return pl.pallas_call(
        functools.partial(_sample_scores_kernel, page=page, pages_per_block=ppb, nb=nb, steps=steps),
        grid_spec=grid_spec,
        out_shape=[jax.ShapeDtypeStruct((db, steps, N_HEADS, span), F32),
                   jax.ShapeDtypeStruct((db, N_HEADS, LANES), F32),
                   jax.ShapeDtypeStruct((db, 1, LANES), I32)],
        compiler_params=_cparams(("parallel", "arbitrary")),
        name="sample_scores",
    )(pt_flat, qcol, r3(q), r3(k_new), *([cache_kt] * PAGES_PER_STEP))


def _sample_attend_kernel(pages_ref, need_ref, vncol_ref, stat_ref, p_ref, *refs, page, pages_per_block, nb, steps):
    del pages_ref
    v_refs = refs[:PAGES_PER_STEP]
    o_ref, acc_ref = refs[PAGES_PER_STEP:]
    b = pl.program_id(0)
    st = pl.program_id(1)
    bps = PAGES_PER_STEP // pages_per_block

    @pl.when(st == 0)
    def _():
        acc_ref[...] = jnp.zeros_like(acc_ref)

    p_step = p_ref[0, 0]
    for bi in range(bps):
        @pl.when(need_ref[b * nb + st * bps + bi] != 0)
        def _(bi=bi):
            for h in range(N_HEADS):
                acc = acc_ref[h]
                for r in range(pages_per_block):
                    pg = bi * pages_per_block + r
                    acc = acc + v_refs[pg][0, h] * p_step[h:h + 1, pg * page:(pg + 1) * page]
                acc_ref[h] = acc

    @pl.when(st == steps - 1)
    def _():
        stat = stat_ref[0]
        for h in range(N_HEADS):
            tot = jnp.sum(acc_ref[h], axis=-1, keepdims=True)
            o_ref[0, h] = (tot + stat[h:h + 1, 1:2] * vncol_ref[0, h]) / stat[h:h + 1, 0:1]


def _fetched_pages(page_table, need, pages_per_block):
    db, n_pages = page_table.shape
    steps = n_pages // PAGES_PER_STEP
    needed = (jnp.repeat(need, pages_per_block, axis=1) != 0).reshape(db, steps, PAGES_PER_STEP)
    step_id = jnp.arange(steps, dtype=I32)[None, :, None]
    last = lax.cummax(jnp.where(needed, step_id, 0), axis=1)
    return jnp.take_along_axis(page_table.reshape(db, steps, PAGES_PER_STEP), last, axis=1).reshape(-1)


def _sample_attend(vncol, probs, stat, need, cache_vt, page_table):
    db, n_pages = page_table.shape
    page = cache_vt.shape[-1]
    ppb = MOBA_BLOCK // page
    steps = n_pages // PAGES_PER_STEP
    nb = n_pages // ppb
    span = PAGES_PER_STEP * page
    need2 = need.reshape(db, LANES)[:, :nb]
    col = lambda: pl.BlockSpec((1, N_HEADS, HEAD_DIM, 1), lambda b, s, *_: (b, 0, 0, 0))
    grid_spec = pltpu.PrefetchScalarGridSpec(
        num_scalar_prefetch=2,
        grid=(db, steps),
        in_specs=[col(),
                  pl.BlockSpec((1, N_HEADS, LANES), lambda b, s, *_: (b, 0, 0)),
                  pl.BlockSpec((1, 1, N_HEADS, span), lambda b, s, *_: (b, s, 0, 0))] + _page_specs(n_pages),
        out_specs=col(),
        scratch_shapes=[pltpu.VMEM((N_HEADS, HEAD_DIM, page), F32)],
    )
    out = pl.pallas_call(
        functools.partial(_sample_attend_kernel, page=page, pages_per_block=ppb, nb=nb, steps=steps),
        grid_spec=grid_spec,
        out_shape=jax.ShapeDtypeStruct((db, N_HEADS, HEAD_DIM, 1), F32),
        compiler_params=_cparams(("parallel", "arbitrary")),
        name="sample_attend",
    )(_fetched_pages(page_table, need2, ppb), need2.reshape(-1), vncol, stat, probs, *([cache_vt] * PAGES_PER_STEP))
    return out.reshape(db, N_HEADS * HEAD_DIM)


CONV_HALO = 32


def _ln_silu(y, g, b):
    mu = jnp.mean(y, axis=-1, keepdims=True)
    var = jnp.mean(jnp.square(y - mu), axis=-1, keepdims=True)
    n = (y - mu) * lax.rsqrt(var + EPS) * g + b
    return n * _sigmoid(n)


def _conv_prompt_kernel(zc_ref, zp_ref, dw_ref, db_ref, g_ref, b_ref, o_ref, win_ref, *, cw, tile):
    i = pl.program_id(1)
    prev = zp_ref[0, tile - CONV_HALO:tile, :]
    win_ref[0:CONV_HALO, :] = jnp.where(i > 0, prev, 0.0)
    win_ref[CONV_HALO:CONV_HALO + tile, :] = zc_ref[0]
    off = CONV_HALO - (cw - 1)
    y = jnp.broadcast_to(db_ref[...], (tile, db_ref.shape[-1]))
    for w in range(cw):
        y = y + win_ref[off + w:off + w + tile, :] * dw_ref[w:w + 1, :]
    o_ref[0] = _ln_silu(y, g_ref[...], b_ref[...])


def _conv_prompt(z, dw_w, dw_b, ln_g, ln_b, tile=256):
    b, s, c = z.shape
    cw = dw_w.shape[0]
    assert s % tile == 0 and cw - 1 <= CONV_HALO <= tile
    full = lambda a: pl.BlockSpec(a.shape, lambda bi, i: (0,) * a.ndim)
    return pl.pallas_call(
        functools.partial(_conv_prompt_kernel, cw=cw, tile=tile),
        grid=(b, s // tile),
        in_specs=[pl.BlockSpec((1, tile, c), lambda bi, i: (bi, i, 0)),
                  pl.BlockSpec((1, tile, c), lambda bi, i: (bi, jnp.maximum(i - 1, 0), 0)),
                  full(dw_w), full(dw_b), full(ln_g), full(ln_b)],
        out_specs=pl.BlockSpec((1, tile, c), lambda bi, i: (bi, i, 0)),
        out_shape=jax.ShapeDtypeStruct((b, s, c), F32),
        scratch_shapes=[pltpu.VMEM((CONV_HALO + tile, c), F32)],
        compiler_params=_cparams(("parallel", "arbitrary")),
        name="conv_prompt",
    )(z, z, dw_w, dw_b, ln_g, ln_b)


SAMPLE_CONV_ROWS = 8


def _conv_sample_kernel(st_ref, z_ref, dw_ref, db_ref, g_ref, b_ref, o_ref, ns_ref, *, cw):
    rows = []
    for r in range(SAMPLE_CONV_ROWS):
        y = jnp.sum(st_ref[r] * dw_ref[0:cw - 1, :], axis=0, keepdims=True)
        rows.append(y + z_ref[r:r + 1, :] * dw_ref[cw - 1:cw, :] + db_ref[...])
        ns_ref[r, 0:cw - 2, :] = st_ref[r, 1:cw - 1, :]
        ns_ref[r, cw - 2:cw - 1, :] = z_ref[r:r + 1, :]
    o_ref[...] = _ln_silu(jnp.concatenate(rows, axis=0), g_ref[...], b_ref[...])


def _conv_sample(state, z, dw_w, dw_b, ln_g, ln_b):
    db, sw, c = state.shape
    cw = dw_w.shape[0]
    r = SAMPLE_CONV_ROWS
    assert sw == cw - 1 and db % r == 0
    full = lambda a: pl.BlockSpec(a.shape, lambda i: (0,) * a.ndim)
    return pl.pallas_call(
        functools.partial(_conv_sample_kernel, cw=cw),
        grid=(db // r,),
        in_specs=[pl.BlockSpec((r, sw, c), lambda i: (i, 0, 0)), pl.BlockSpec((r, c), lambda i: (i, 0)),
                  full(dw_w), full(dw_b), full(ln_g), full(ln_b)],
        out_specs=[pl.BlockSpec((r, c), lambda i: (i, 0)), pl.BlockSpec((r, sw, c), lambda i: (i, 0, 0))],
        out_shape=[jax.ShapeDtypeStruct((db, c), F32), jax.ShapeDtypeStruct((db, sw, c), F32)],
        compiler_params=_cparams(("parallel",)),
        name="conv_sample",
    )(state, z, dw_w, dw_b, ln_g, ln_b)


def _outproj_kernel(x_ref, att_ref, conv_ref, sga_ref, sgb_ref, wao_ref, wco_ref, wo_ref, g2_ref, wq_ref,
                    x1_ref, hn_ref, pq_ref):
    merged = sga_ref[...] * _dot(att_ref[...], wao_ref[...]) + sgb_ref[...] * _dot(conv_ref[...], wco_ref[...])
    x1 = x_ref[...] + _dot(merged, wo_ref[...])
    x1_ref[...] = x1
    hn = _rms(x1, g2_ref[...])
    hn_ref[...] = hn
    pq_ref[...] = _dot(hn, wq_ref[...])


def _outproj(x2, att, conv, sga, sgb, wao, wco, wo, g2, wq, tm):
    t, d = x2.shape
    assert t % tm == 0
    row = lambda a: pl.BlockSpec((tm, a.shape[1]), lambda i: (i, 0))
    full = lambda a: pl.BlockSpec(a.shape, lambda i: (0,) * a.ndim)
    acts = (x2, att, conv, sga, sgb)
    wts = (wao, wco, wo, g2, wq)
    pq_w = wq.shape[1]
    return pl.pallas_call(
        _outproj_kernel,
        grid=(t // tm,),
        in_specs=[row(a) for a in acts] + [full(a) for a in wts],
        out_specs=[pl.BlockSpec((tm, d), lambda i: (i, 0)), pl.BlockSpec((tm, d), lambda i: (i, 0)),
                   pl.BlockSpec((tm, pq_w), lambda i: (i, 0))],
        out_shape=[jax.ShapeDtypeStruct((t, d), F32), jax.ShapeDtypeStruct((t, d), F32),
                   jax.ShapeDtypeStruct((t, pq_w), F32)],
        compiler_params=_cparams(("parallel",)),
        name="outproj",
    )(*acts, *wts)


PEER_TILE = 128
CAND_ROWS = PEER_TOPK + 8 * SUBLANES


def _peer_route_kernel(pq_ref, k12_ref, eidx_ref, pidx_ref, shift_ref, gate_ref, s_ref, c_ref, ci_ref, *, n_keys):
    kk = PEER_TOPK
    t = PEER_TILE
    nh = PEER_HEADS
    half = kk // 2
    assert kk == 16 and half == SUBLANES
    key_id = lax.broadcasted_iota(I32, (n_keys, t), 0)
    kidx = lax.broadcasted_iota(I32, (kk, t), 0)
    for h in range(nh):
        qh = pq_ref[:, h * LANES:(h + 1) * LANES]
        s12 = _dot_nt(k12_ref[...], qh, HIGHEST)
        s_ref[2 * h] = s12[0:n_keys]
        s_ref[2 * h + 1] = s12[n_keys:2 * n_keys]

    def level1(it, carry):
        vals, idxs = carry
        nv, ni = [], []
        for c in range(2 * nh):
            s = s_ref[c]
            mx = jnp.max(s, axis=0, keepdims=True)
            pick = jnp.min(jnp.where(s == mx, key_id, n_keys), axis=0, keepdims=True)
            s_ref[c] = jnp.where(key_id == pick, -jnp.inf, s)
            nv.append(jnp.where(kidx == it, mx, vals[c]))
            ni.append(jnp.where(kidx == it, pick, idxs[c]))
        return tuple(nv), tuple(ni)

    zf = tuple(jnp.zeros((kk, t), F32) for _ in range(2 * nh))
    zi = tuple(jnp.zeros((kk, t), I32) for _ in range(2 * nh))
    vals, idxs = lax.fori_loop(0, kk, level1, (zf, zi))

    b16 = lax.broadcasted_iota(I32, (kk, t), 0)
    b8 = lax.broadcasted_iota(I32, (half, t), 0)
    flat_parts = [b16, kk + b8]
    for a in range(2, half):
        flat_parts.append(a * kk + b8)
    flat_parts.append((half + b8) * kk)
    flat = jnp.concatenate(flat_parts, axis=0)
    for h in range(nh):
        v1, v2, i1, i2 = vals[2 * h], vals[2 * h + 1], idxs[2 * h], idxs[2 * h + 1]
        c_parts = [v1[0:1] + v2, v1[1:2] + v2[0:half]]
        i_parts = [i1[0:1] * n_keys + i2, i1[1:2] * n_keys + i2[0:half]]
        for a in range(2, half):
            c_parts.append(jnp.where(b8 < kk // (a + 1), v1[a:a + 1] + v2[0:half], -jnp.inf))
            i_parts.append(i1[a:a + 1] * n_keys + i2[0:half])
        c_parts.append(v1[half:kk] + v2[0:1])
        i_parts.append(i1[half:kk] * n_keys + i2[0:1])
        c_ref[h] = jnp.concatenate(c_parts, axis=0)
        ci_ref[h] = jnp.concatenate(i_parts, axis=0)

    def level2(it, carry):
        vals2, exps = carry
        nv, ne = [], []
        for h in range(nh):
            c = c_ref[h]
            mx = jnp.max(c, axis=0, keepdims=True)
            pick = jnp.min(jnp.where(c == mx, flat, kk * kk), axis=0, keepdims=True)
            hit = flat == pick
            e = jnp.max(jnp.where(hit, ci_ref[h], -1), axis=0, keepdims=True)
            c_ref[h] = jnp.where(hit, -jnp.inf, c)
            nv.append(jnp.where(kidx == it, mx, vals2[h]))
            ne.append(jnp.where(kidx == it, e, exps[h]))
        return tuple(nv), tuple(ne)

    sc, eidx = lax.fori_loop(0, kk, level2, (zf[:nh], zi[:nh]))
    for h in range(nh):
        ex = jnp.exp(sc[h] - jnp.max(sc[h], axis=0, keepdims=True))
        rows = slice(h * kk, (h + 1) * kk)
        gate_ref[0, rows, :] = ex / jnp.sum(ex, axis=0, keepdims=True)
        eidx_ref[0, rows, :] = eidx[h]
        pidx_ref[0, rows, :] = eidx[h] >> 1
        shift_ref[0, rows, :] = ((eidx[h] & 1) ^ 1) << 4


def _peer_route(pq, k1ext, k2ext):
    t, w = pq.shape
    n_keys = k1ext.shape[0]
    slots = PEER_HEADS * PEER_TOPK
    assert t % PEER_TILE == 0 and w == PEER_HEADS * LANES and k1ext.shape[1] == LANES
    k12 = jnp.concatenate([k1ext, k2ext], axis=0)
    nt = t // PEER_TILE
    full = lambda a: pl.BlockSpec(a.shape, lambda i: (0,) * a.ndim)
    ospec = pl.BlockSpec((1, slots, PEER_TILE), lambda i: (i, 0, 0))
    ishape = jax.ShapeDtypeStruct((nt, slots, PEER_TILE), I32)
    return pl.pallas_call(
        functools.partial(_peer_route_kernel, n_keys=n_keys),
        grid=(nt,),
        in_specs=[pl.BlockSpec((PEER_TILE, w), lambda i: (i, 0)), full(k12)],
        out_specs=[ospec, ospec, ospec, ospec],
        out_shape=[ishape, ishape, ishape, jax.ShapeDtypeStruct((nt, slots, PEER_TILE), F32)],
        scratch_shapes=[pltpu.VMEM((2 * PEER_HEADS, n_keys, PEER_TILE), F32),
                        pltpu.VMEM((PEER_HEADS, CAND_ROWS, PEER_TILE), F32),
                        pltpu.VMEM((PEER_HEADS, CAND_ROWS, PEER_TILE), I32)],
        compiler_params=_cparams(("parallel",)),
        name="peer_route",
    )(pq, k12)


HI_MASK = np.uint32(0xFFFF0000)
HALF_SUB = SUBLANES // 2


def _as_f32(bits):
    return lax.bitcast_convert_type(bits, F32)


def _tile_copies(hbm_refs, smem_refs, sems, tile, slot, n):
    dst = pl.ds(pl.multiple_of(slot * n, n), n)
    return [pltpu.make_async_copy(h.at[tile], s.at[dst], sems.at[k, slot])
            for k, (h, s) in enumerate(zip(hbm_refs, smem_refs))]


def _stream_tiles(hbm_refs, smem_refs, sems, n_tiles, n):
    i = pl.program_id(0)
    slot = i % 2

    @pl.when(i == 0)
    def _():
        for c in _tile_copies(hbm_refs, smem_refs, sems, 0, 0, n):
            c.start()

    @pl.when(i + 1 < n_tiles)
    def _():
        for c in _tile_copies(hbm_refs, smem_refs, sems, i + 1, 1 - slot, n):
            c.start()

    for c in _tile_copies(hbm_refs, smem_refs, sems, i, slot, n):
        c.wait()
    return slot


SCORE_CHUNK_GROUPS = 8


def _peer_score_kernel(pidx_hbm, shift_hbm, x_ref, gate_ref, tab_ref, act_ref, pidx_sm, shift_sm, dots_ref, grp_ref, sems,
                       *, n_tiles, slots):
    tt = PEER_TILE
    n = slots * tt
    slot = _stream_tiles((pidx_hbm, shift_hbm), (pidx_sm, shift_sm), sems, n_tiles, n)
    lane = lax.broadcasted_iota(I32, (slots, tt), 1)
    dots_ref[...] = jnp.zeros_like(dots_ref)
    chunk_rows = SCORE_CHUNK_GROUPS * SUBLANES
    span = chunk_rows * tt
    view_len = 2 * n - (chunk_rows - 1) * tt

    lane_c = lax.broadcasted_iota(I32, (chunk_rows, tt), 1)
    grp_ref[...] = jnp.zeros_like(grp_ref)

    def finish(tok, buf, c):
        rows = pl.ds(pl.multiple_of(c * chunk_rows, chunk_rows), chunk_rows)
        part = grp_ref[buf, pl.ds(c * SCORE_CHUNK_GROUPS, SCORE_CHUNK_GROUPS)]
        col = jnp.sum(part.reshape(chunk_rows, LANES), axis=-1, keepdims=True)
        dots_ref[rows, :] = jnp.where(lane_c == tok, col, dots_ref[rows, :])

    def token(t, carry):
        base = slot * n + t
        x = x_ref[t]
        cur = t % 2

        def chunk(c, inner):
            finish(t - 1, 1 - cur, c)
            cbase = base + c * span
            for g in range(SCORE_CHUNK_GROUPS):
                prods = []
                for r in range(SUBLANES):
                    view = pl.ds((g * SUBLANES + r) * tt, view_len)
                    w = tab_ref[pidx_sm.at[view][cbase]]
                    prods.append(_as_f32((w << shift_sm.at[view][cbase].astype(U32)) & HI_MASK) * x)
                grp_ref[cur, c * SCORE_CHUNK_GROUPS + g] = _sublane_sums(prods)
            return inner

        lax.fori_loop(0, slots // chunk_rows, chunk, 0)
        return carry

    lax.fori_loop(0, tt, token, 0)
    for c in range(slots // chunk_rows):
        finish(tt - 1, (tt - 1) % 2, c)
    act_ref[0] = _gelu_erf(dots_ref[...]) * gate_ref[0]


def _peer_scores(x3, pidx, shift, gate, tab):
    nt, slots, tt = gate.shape
    n = slots * tt
    flat = lambda a: a.reshape(nt, n)
    kern = functools.partial(_peer_score_kernel, n_tiles=nt, slots=slots)
    return pl.pallas_call(
        kern,
        grid=(nt,),
        in_specs=[pl.BlockSpec(memory_space=pl.ANY), pl.BlockSpec(memory_space=pl.ANY),
                  pl.BlockSpec((tt, SUBLANES, LANES), lambda i: (i, 0, 0)),
                  pl.BlockSpec((1, slots, tt), lambda i: (i, 0, 0)),
                  pl.BlockSpec(memory_space=pltpu.VMEM)],
        out_specs=pl.BlockSpec((1, slots, tt), lambda i: (i, 0, 0)),
        out_shape=jax.ShapeDtypeStruct((nt, slots, tt), F32),
        scratch_shapes=[pltpu.SMEM((2 * n,), I32), pltpu.SMEM((2 * n,), I32),
                        pltpu.VMEM((slots, tt), F32), pltpu.VMEM((2, slots // SUBLANES, SUBLANES, LANES), F32),
                        pltpu.SemaphoreType.DMA((2, 2))],
        compiler_params=_cparams(("arbitrary",)),
        name="peer_scores",
    )(flat(pidx), flat(shift), x3, gate, tab)


MIX_ACCS = 4


MIX_CHUNK = 64


def _peer_mix_kernel(eidx_hbm, act_hbm, tab_ref, o_ref, eidx_sm, act_sm, sems, *, n_tiles, slots):
    tt = PEER_TILE
    n = slots * tt
    slot = _stream_tiles((eidx_hbm, act_hbm), (eidx_sm, act_sm), sems, n_tiles, n)
    span = MIX_CHUNK * tt
    view_len = 2 * n - (MIX_CHUNK - 1) * tt

    def token(t, carry):
        base = slot * n + t

        def chunk(c, accs):
            lo, hi = list(accs[0]), list(accs[1])
            cbase = base + c * span
            for r in range(MIX_CHUNK):
                view = pl.ds(r * tt, view_len)
                w = tab_ref[eidx_sm.at[view][cbase]]
                a = act_sm.at[view][cbase]
                lo[r % MIX_ACCS] = lo[r % MIX_ACCS] + a * _as_f32(w << 16)
                hi[r % MIX_ACCS] = hi[r % MIX_ACCS] + a * _as_f32(w & HI_MASK)
            return tuple(lo), tuple(hi)

        zero = tuple(jnp.zeros((HALF_SUB, LANES), F32) for _ in range(MIX_ACCS))
        lo, hi = lax.fori_loop(0, slots // MIX_CHUNK, chunk, (zero, zero))
        o_ref[t, 0:HALF_SUB, :] = (lo[0] + lo[1]) + (lo[2] + lo[3])
        o_ref[t, HALF_SUB:SUBLANES, :] = (hi[0] + hi[1]) + (hi[2] + hi[3])
        return carry

    lax.fori_loop(0, tt, token, 0)


def _peer_mix(eidx, act, tab):
    nt, slots, tt = act.shape
    n = slots * tt
    flat = lambda a: a.reshape(nt, n)
    kern = functools.partial(_peer_mix_kernel, n_tiles=nt, slots=slots)
    return pl.pallas_call(
        kern,
        grid=(nt,),
        in_specs=[pl.BlockSpec(memory_space=pl.ANY), pl.BlockSpec(memory_space=pl.ANY),
                  pl.BlockSpec(memory_space=pltpu.VMEM)],
        out_specs=pl.BlockSpec((tt, SUBLANES, LANES), lambda i: (i, 0, 0)),
        out_shape=jax.ShapeDtypeStruct((nt * tt, SUBLANES, LANES), F32),
        scratch_shapes=[pltpu.SMEM((2 * n,), I32), pltpu.SMEM((2 * n,), F32), pltpu.SemaphoreType.DMA((2, 2))],
        compiler_params=_cparams(("arbitrary",)),
        name="peer_mix",
    )(flat(eidx), flat(act), tab)


def _bf16_bits(tab):
    return lax.bitcast_convert_type(tab.astype(BF16), jnp.uint16).astype(U32)


def _pack_pairs(tab):
    n, d = tab.shape
    assert n % 2 == 0 and d == SUBLANES * LANES
    bits = _bf16_bits(tab).reshape(n // 2, 2, d)
    return (bits[:, 0] | (bits[:, 1] << 16)).reshape(n // 2, SUBLANES, LANES)


def _pack_halves(tab):
    n, d = tab.shape
    assert d == SUBLANES * LANES
    bits = _bf16_bits(tab)
    return (bits[:, :d // 2] | (bits[:, d // 2:] << 16)).reshape(n, HALF_SUB, LANES)


def _peer(hn, pq, k1ext, k2ext, u_pack, v_pack):
    t, d = hn.shape
    eidx, pidx, shift, gate = _peer_route(pq, k1ext, k2ext)
    act = _peer_scores(hn.reshape(t, SUBLANES, LANES), pidx, shift, gate, u_pack)
    return _peer_mix(eidx, act, v_pack).reshape(t, d)


def _ple_kernel(x1_ref, po_ref, pe_ref, g3_ref, wg_ref, wp_ref, gf_ref, y_ref, *, final):
    x2 = x1_ref[...] + po_ref[...]
    gate = _sigmoid(_dot(_rms(x2, g3_ref[...]), wg_ref[...]))
    x3 = x2 + gate * _dot(pe_ref[...], wp_ref[...])
    y_ref[...] = _rms(x3, gf_ref[...]) if final else x3


def _ple(x1, po, pe, g3, wg, wp, gf, tm, final):
    t, d = x1.shape
    assert t % tm == 0
    row = lambda a: pl.BlockSpec((tm, a.shape[1]), lambda i: (i, 0))
    full = lambda a: pl.BlockSpec(a.shape, lambda i: (0,) * a.ndim)
    return pl.pallas_call(
        functools.partial(_ple_kernel, final=final),
        grid=(t // tm,),
        in_specs=[row(x1), row(po), row(pe), full(g3), full(wg), full(wp), full(gf)],
        out_specs=pl.BlockSpec((tm, d), lambda i: (i, 0)),
        out_shape=jax.ShapeDtypeStruct((t, d), F32),
        compiler_params=_cparams(("parallel",)),
        name="ple_final",
    )(x1, po, pe, g3, wg, wp, gf)


def kernel(x_prompt, x_sample, cache_k, cache_v, state_conv, page_table, p_prompt, p_sample, norm1_g, w_in, dw_w, dw_b, cln_g, cln_b, w_att_out, w_conv_out, w_o, norm2_g, peer_wq, peer_k1, peer_k2, peer_u, peer_v, norm3_g, w_ple, w_ple_gate, final_g):
    depth = w_in.shape[0]
    b, s, d = x_prompt.shape
    db, ds, _ = x_sample.shape
    page = cache_k.shape[2]
    n_pages = page_table.shape[1]
    att_w = N_HEADS * HEAD_DIM
    assert ds == 1 and (n_pages * page) % MOBA_BLOCK == 0
    row = lambda a: a.reshape(1, -1)
    pt_flat = page_table.reshape(-1).astype(I32)
    gf = row(final_g)
    assert 2 * peer_k1.shape[-1] == LANES

    xp = x_prompt.reshape(b * s, d)
    xs = x_sample.reshape(db, d)
    outs = [[] for _ in range(6)]
    for li in range(depth):
        last = li == depth - 1
        bf = lambda a: a[li].astype(BF16)
        w_in_b, wao, wco, wo, wq, wpg, wpl = (bf(a) for a in (w_in, w_att_out, w_conv_out, w_o, peer_wq, w_ple_gate, w_ple))
        zeros = jnp.zeros_like(peer_k1[li])
        k1ext = jnp.concatenate([peer_k1[li], zeros], axis=1)
        k2ext = jnp.concatenate([zeros, peer_k2[li]], axis=1)
        u_pack = _pack_pairs(peer_u[li])
        v_pack = _pack_halves(peer_v[li])
        conv_w = (dw_w[li], row(dw_b[li]), row(cln_g[li]), row(cln_b[li]))
        g1, g2, g3 = row(norm1_g[li]), row(norm2_g[li]), row(norm3_g[li])
        ckt = jnp.transpose(cache_k[li], (0, 2, 3, 1))
        cvt = jnp.transpose(cache_v[li], (0, 2, 3, 1))

        qs, ks, vs, zs, sgas, sgbs = _inproj(xs, g1, w_in_b, db)
        col = lambda a: a.reshape(db, N_HEADS, HEAD_DIM, 1)
        probs, stat, need = _sample_scores(qs, ks, col(qs), ckt, pt_flat, n_pages)
        atts = _sample_attend(col(vs), probs, stat, need, cvt, page_table.astype(I32))
        convs, new_state = _conv_sample(state_conv[li], zs, *conv_w)
        x1s, hns, pqs = _outproj(xs, atts, convs, sgas, sgbs, wao, wco, wo, g2, wq, db)
        pos = _peer(hns, pqs, k1ext, k2ext, u_pack, v_pack)
        xs = _ple(x1s, pos, p_sample[li].reshape(db, -1), g3, wpg, wpl, gf, db, last)
        outs[3].append(ks.reshape(db, ds, N_HEADS, HEAD_DIM))
        outs[4].append(vs.reshape(db, ds, N_HEADS, HEAD_DIM))
        outs[5].append(new_state)

        q, k, v, z, sga, sgb = _inproj(xp, g1, w_in_b, 256)
        att = _moba_prompt(q.reshape(b, s, att_w), k.reshape(b, s, att_w), v.reshape(b, s, att_w))
        z3 = z.reshape(b, s, -1)
        conv = _conv_prompt(z3, *conv_w)
        x1, hn, pq = _outproj(xp, att.reshape(b * s, att_w), conv.reshape(b * s, -1), sga, sgb, wao, wco, wo, g2, wq, 256)
        po = _peer(hn, pq, k1ext, k2ext, u_pack, v_pack)
        xp = _ple(x1, po, p_prompt[li].reshape(b * s, -1), g3, wpg, wpl, gf, 256, last)
        outs[0].append(k.reshape(b, s, N_HEADS, HEAD_DIM))
        outs[1].append(v.reshape(b, s, N_HEADS, HEAD_DIM))
        outs[2].append(z3[:, s - (dw_w.shape[1] - 1):, :])

    return (xp.reshape(b, s, d), xs.reshape(db, ds, d)) + tuple(jnp.stack(o) for o in outs)
```

```python
import functools

import numpy as np
import jax
import jax.numpy as jnp
from jax import lax
from jax.experimental import pallas as pl
from jax.experimental.pallas import tpu as pltpu

F32 = jnp.float32
BF16 = jnp.bfloat16
I32 = jnp.int32
U32 = jnp.uint32

N_HEADS = 8
HEAD_DIM = 64
MOBA_BLOCK = 256
MOBA_TOPK = 3
PEER_HEADS = 8
PEER_TOPK = 16
EPS = 1e-6
NEG = -1e30
ATT_SCALE = HEAD_DIM ** -0.5

LANES = 128
SUBLANES = 8
VMEM_LIMIT_BYTES = 56 * 1024 * 1024

HIGHEST = lax.Precision.HIGHEST


def _cparams(sem):
    return pltpu.CompilerParams(dimension_semantics=sem, vmem_limit_bytes=VMEM_LIMIT_BYTES)


def _rms(x, g):
    return x * lax.rsqrt(jnp.mean(x * x, axis=-1, keepdims=True) + EPS) * g


def _sigmoid(x):
    return 1.0 / (1.0 + jnp.exp(-x))


def _dot(a, b):
    return jnp.dot(a.astype(BF16), b.astype(BF16), preferred_element_type=F32)


def _dot_nt(a, b, precision=None):
    return lax.dot_general(a, b, (((1,), (1,)), ((), ())), preferred_element_type=F32, precision=precision)


def _gelu_erf(x):
    return 0.5 * x * (1.0 + lax.erf(x * np.float32(1.0 / np.sqrt(2.0))))


def _inproj_kernel(x_ref, g_ref, w_ref, q_ref, k_ref, v_ref, z_ref, sga_ref, sgb_ref, *t_refs, att_w, conv_ch, d_model):
    h = _rms(x_ref[...], g_ref[...]).astype(BF16)

    def proj(lo, width):
        return jnp.dot(h, w_ref[:, lo:lo + width], preferred_element_type=F32)

    q_ref[...] = proj(0, att_w)
    k = proj(att_w, att_w)
    v = proj(2 * att_w, att_w)
    k_ref[...] = k
    v_ref[...] = v
    if t_refs:
        kt_ref, vt_ref = t_refs
        kt_ref[0] = k.T.reshape(N_HEADS, HEAD_DIM, k.shape[0])
        vt_ref[0] = v.T.reshape(N_HEADS, HEAD_DIM, v.shape[0])
    base = 3 * att_w
    z_ref[...] = proj(base, conv_ch) * _sigmoid(proj(base + conv_ch, conv_ch))
    base += 2 * conv_ch
    sga_ref[...] = _sigmoid(proj(base, d_model))
    sgb_ref[...] = _sigmoid(proj(base + d_model, d_model))


def _inproj(x2, g, w_bf, tm, seq_len=None):
    t, d = x2.shape
    att_w = N_HEADS * HEAD_DIM
    conv_ch = d // 2
    assert w_bf.shape[1] == 3 * att_w + 2 * conv_ch + 2 * d and t % tm == 0
    row = lambda w: pl.BlockSpec((tm, w), lambda i: (i, 0))
    full = lambda a: pl.BlockSpec(a.shape, lambda i: (0,) * a.ndim)
    out_w = (att_w, att_w, att_w, conv_ch, d, d)
    out_specs = [row(w) for w in out_w]
    out_shape = [jax.ShapeDtypeStruct((t, w), F32) for w in out_w]
    if seq_len is not None:
        assert seq_len % tm == 0 and t % seq_len == 0 and tm % LANES == 0
        tiles = seq_len // tm
        t_spec = pl.BlockSpec((1, N_HEADS, HEAD_DIM, tm), lambda i: (i // tiles, 0, 0, i % tiles))
        t_shape = jax.ShapeDtypeStruct((t // seq_len, N_HEADS, HEAD_DIM, seq_len), F32)
        out_specs += [t_spec, t_spec]
        out_shape += [t_shape, t_shape]
    return pl.pallas_call(
        functools.partial(_inproj_kernel, att_w=att_w, conv_ch=conv_ch, d_model=d),
        grid=(t // tm,),
        in_specs=[row(d), full(g), full(w_bf)],
        out_specs=out_specs,
        out_shape=out_shape,
        compiler_params=_cparams(("parallel",)),
        name="inproj",
    )(x2, g, w_bf)


def _moba_prompt_kernel(q_ref, k_ref, v_ref, o_ref, kmean_ref, sel_ref, *, nb, rq):
    blk = MOBA_BLOCK
    i = pl.program_id(2)
    own = (i * rq) // blk
    off = (i * rq) % blk

    @pl.when(i == 0)
    def _():
        for j in range(nb):
            kmean_ref[j:j + 1, :] = jnp.sum(k_ref[0, j * blk:(j + 1) * blk, :], axis=0, keepdims=True) * (1.0 / blk)

    lane = lax.broadcasted_iota(I32, (1, LANES), 1)
    row_i = lax.broadcasted_iota(I32, (rq, blk), 0) + off
    col_i = lax.broadcasted_iota(I32, (rq, blk), 1)
    causal = col_i <= row_i
    kmean = kmean_ref[...]
    bidx = lax.broadcasted_iota(I32, (nb, rq), 0)
    expand = (lax.broadcasted_iota(I32, (LANES, nb * LANES), 1) // LANES
              == lax.broadcasted_iota(I32, (LANES, nb * LANES), 0)).astype(BF16)
    qbs = []
    for hs in range(2):
        hm = (lane < HEAD_DIM) if hs == 0 else (lane >= HEAD_DIM)
        q = jnp.where(hm, q_ref[0], 0.0)
        gates = jnp.where(bidx < own, _dot_nt(kmean, q, HIGHEST), NEG)
        rank = jnp.zeros((nb, rq), F32)
        for m in range(nb):
            gm = gates[m:m + 1, :]
            ahead = ((bidx > m) & (gm >= gates)) | ((bidx < m) & (gm > gates))
            rank = rank + ahead.astype(F32)
        sel_t = jnp.where((rank < MOBA_TOPK) & (bidx < own), 1.0, 0.0)
        sel_q = jnp.concatenate([sel_t, jnp.zeros((LANES - nb, rq), F32)], axis=0).T
        sel_all = _dot(sel_q, expand)
        for j in range(nb):
            sel_ref[hs, j] = sel_all[:, j * LANES:(j + 1) * LANES]
        qbs.append(q.astype(BF16))

    def step(s, mask, vb, carry):
        m_run, l_run, acc = carry
        s = jnp.where(mask, s * ATT_SCALE, NEG)
        m_new = jnp.maximum(m_run, jnp.max(s, axis=-1, keepdims=True))
        p = jnp.where(mask, jnp.exp(s - m_new), 0.0)
        alpha = jnp.exp(m_run - m_new)
        l_new = alpha * l_run + jnp.sum(p, axis=-1, keepdims=True)
        acc_new = alpha * acc + jnp.dot(p.astype(BF16), vb, preferred_element_type=F32)
        return m_new, l_new, acc_new

    def both_heads(j, masks, carries):
        start = pl.multiple_of(j * blk, blk)
        kb = k_ref[0, pl.ds(start, blk), :].astype(BF16)
        vb = v_ref[0, pl.ds(start, blk), :].astype(BF16)
        return tuple(step(_dot_nt(qbs[hs], kb), masks[hs], vb, carries[hs]) for hs in range(2))

    def past_body(j, carries):
        masks = []
        for hs in range(2):
            sj = sel_ref[hs, j]
            masks.append(jnp.concatenate([sj, sj], axis=1) > 0.5)
        return both_heads(j, masks, carries)

    init = (jnp.full((rq, 1), NEG, F32), jnp.zeros((rq, 1), F32), jnp.zeros((rq, LANES), F32))
    carries = lax.fori_loop(0, own, past_body, (init, init))
    (_, l_a, acc_a), (_, l_b, acc_b) = both_heads(own, (causal, causal), carries)
    o_ref[0] = jnp.where(lane < HEAD_DIM, acc_a / l_a, acc_b / l_b)


MOBA_Q_ROWS = 256


def _moba_prompt(q, k, v):
    b, s, w = q.shape
    blk = MOBA_BLOCK
    rq = MOBA_Q_ROWS
    nb = s // blk
    assert s % blk == 0 and blk % rq == 0 and rq % LANES == 0 and nb >= MOBA_TOPK
    assert w % LANES == 0 and LANES == 2 * HEAD_DIM and blk == 2 * LANES
    kv_spec = pl.BlockSpec((1, s, LANES), lambda bi, hp, i: (bi, 0, hp))
    q_spec = pl.BlockSpec((1, rq, LANES), lambda bi, hp, i: (bi, i, hp))
    return pl.pallas_call(
        functools.partial(_moba_prompt_kernel, nb=nb, rq=rq),
        grid=(b, w // LANES, s // rq),
        in_specs=[q_spec, kv_spec, kv_spec],
        out_specs=q_spec,
        out_shape=jax.ShapeDtypeStruct((b, s, w), F32),
        scratch_shapes=[pltpu.VMEM((nb, LANES), F32), pltpu.VMEM((2, nb, rq, LANES), F32)],
        compiler_params=_cparams(("parallel", "parallel", "arbitrary")),
        name="moba_prompt",
    )(q, k, v)


PAGES_PER_STEP = 16


def _sublane_sums(vs):
    sub = lax.broadcasted_iota(I32, (SUBLANES, LANES), 0)
    cur = [vs[_BUTTERFLY_ORDER[p]] for p in range(SUBLANES)]
    for shift, keep in ((1, sub % 2 == 0), (2, sub % 4 < 2), (4, sub < 4)):
        nxt = []
        for a, b in zip(cur[0::2], cur[1::2]):
            nxt.append(jnp.where(keep, a, b) + pltpu.roll(jnp.where(keep, b, a), shift, 0))
        cur = nxt
    return cur[0]


def _butterfly_order():
    ids = [np.full((SUBLANES,), i) for i in range(SUBLANES)]
    sub = np.arange(SUBLANES)
    for shift, keep in ((1, sub % 2 == 0), (2, sub % 4 < 2), (4, sub < 4)):
        nxt = []
        for a, b in zip(ids[0::2], ids[1::2]):
            first = np.where(keep, a, b)
            assert (first == np.roll(np.where(keep, b, a), shift)).all()
            nxt.append(first)
        ids = nxt
    order = np.empty(SUBLANES, np.int64)
    order[ids[0]] = np.arange(SUBLANES)
    return [int(v) for v in order]


_BUTTERFLY_ORDER = _butterfly_order()


def _head_rows(row):
    w = row.shape[-1]
    lane_head = lax.broadcasted_iota(I32, (N_HEADS, w), 1) // HEAD_DIM
    sub = lax.broadcasted_iota(I32, (N_HEADS, w), 0)
    return jnp.where(lane_head == sub, jnp.broadcast_to(row, (N_HEADS, w)), 0.0)


def _sample_scores_kernel(pt_ref, qcol_ref, q_ref, kn_ref, *refs, page, pages_per_block, nb, steps):
    del pt_ref
    k_refs = refs[:PAGES_PER_STEP]
    p_ref, stat_ref, need_ref, qb_ref, gacc_ref = refs[PAGES_PER_STEP:]
    st = pl.program_id(1)
    bps = PAGES_PER_STEP // pages_per_block
    span = PAGES_PER_STEP * page

    @pl.when(st == 0)
    def _():
        qb_ref[...] = jnp.broadcast_to(qcol_ref[0], qb_ref.shape)
        gacc_ref[...] = jnp.zeros_like(gacc_ref)

    lane = lax.broadcasted_iota(I32, (N_HEADS, LANES), 1)
    raws = []
    for p in range(PAGES_PER_STEP):
        per_head = []
        for h in range(N_HEADS):
            prod = k_refs[p][0, h] * qb_ref[h]
            part = prod[0:SUBLANES]
            for c in range(1, HEAD_DIM // SUBLANES):
                part = part + prod[c * SUBLANES:(c + 1) * SUBLANES]
            per_head.append(part)
        raw = _sublane_sums(per_head)
        raws.append(raw)
        p_ref[0, st, :, p * page:(p + 1) * page] = raw * ATT_SCALE
    gacc = gacc_ref[...]
    for bi in range(bps):
        tot = raws[bi * pages_per_block]
        for r in range(1, pages_per_block):
            tot = tot + raws[bi * pages_per_block + r]
        gacc = jnp.where(lane == st * bps + bi, jnp.sum(tot, axis=-1, keepdims=True), gacc)
    gacc_ref[...] = gacc

    @pl.when(st == steps - 1)
    def _():
        gate = jnp.where(lane < nb, gacc * (1.0 / MOBA_BLOCK), -jnp.inf)
        sel = jnp.zeros((N_HEADS, LANES), F32)
        for _ in range(MOBA_TOPK):
            mx = jnp.max(gate, axis=-1, keepdims=True)
            pick = jnp.min(jnp.where(gate == mx, lane, LANES), axis=-1, keepdims=True)
            hit = lane == pick
            sel = jnp.where(hit, 1.0, sel)
            gate = jnp.where(hit, -jnp.inf, gate)
        n_keys = nb * MOBA_BLOCK
        expand = (lax.broadcasted_iota(I32, (LANES, n_keys), 1) // MOBA_BLOCK
                  == lax.broadcasted_iota(I32, (LANES, n_keys), 0))
        mask = _dot(sel, expand.astype(F32)) > 0.5
        masks = [mask[:, t * span:(t + 1) * span] for t in range(steps)]
        s_new = jnp.sum(_head_rows(q_ref[0]) * kn_ref[0], axis=-1, keepdims=True) * ATT_SCALE
        m = s_new
        for t in range(steps):
            m = jnp.maximum(m, jnp.max(jnp.where(masks[t], p_ref[0, t], NEG), axis=-1, keepdims=True))
        pn = jnp.exp(s_new - m)
        l = pn
        for t in range(steps):
            p = jnp.where(masks[t], jnp.exp(p_ref[0, t] - m), 0.0)
            l = l + jnp.sum(p, axis=-1, keepdims=True)
            p_ref[0, t] = p
        stat_ref[0] = jnp.where(lane == 0, l, jnp.where(lane == 1, pn, 0.0))
        need_ref[0] = jnp.max(sel, axis=0, keepdims=True).astype(I32)


def _page_specs(n_pages):
    def make(p):
        return pl.BlockSpec((1, N_HEADS, HEAD_DIM, LANES),
                            lambda b, s, pages, *_: (pages[b * n_pages + s * PAGES_PER_STEP + p], 0, 0, 0))
    return [make(p) for p in range(PAGES_PER_STEP)]


def _sample_scores(q, k_new, qcol, cache_kt, pt_flat, n_pages):
    db, w = q.shape
    page = cache_kt.shape[-1]
    ppb = MOBA_BLOCK // page
    steps = n_pages // PAGES_PER_STEP
    nb = n_pages // ppb
    span = PAGES_PER_STEP * page
    assert page == LANES and n_pages % PAGES_PER_STEP == 0 and PAGES_PER_STEP % ppb == 0 and MOBA_TOPK <= nb <= LANES
    one = lambda: pl.BlockSpec((1, 1, w), lambda b, s, pt: (b, 0, 0))
    grid_spec = pltpu.PrefetchScalarGridSpec(
        num_scalar_prefetch=1,
        grid=(db, steps),
        in_specs=[pl.BlockSpec((1, N_HEADS, HEAD_DIM, 1), lambda b, s, pt: (b, 0, 0, 0)), one(), one()]
        + _page_specs(n_pages),
        out_specs=[pl.BlockSpec((1, steps, N_HEADS, span), lambda b, s, pt: (b, 0, 0, 0)),
                   pl.BlockSpec((1, N_HEADS, LANES), lambda b, s, pt: (b, 0, 0)),
                   pl.BlockSpec((1, 1, LANES), lambda b, s, pt: (b, 0, 0))],
        scratch_shapes=[pltpu.VMEM((N_HEADS, HEAD_DIM, page), F32), pltpu.VMEM((N_HEADS, LANES), F32)],
    )
    r3 = lambda a: a.reshape(db, 1, w)
    return pl.pallas_call(
        functools.partial(_sample_scores_kernel, page=page, pages_per_block=ppb, nb=nb, steps=steps),
        grid_spec=grid_spec,
        out_shape=[jax.ShapeDtypeStruct((db, steps, N_HEADS, span), F32),
                   jax.ShapeDtypeStruct((db, N_HEADS, LANES), F32),
                   jax.ShapeDtypeStruct((db, 1, LANES), I32)],
        compiler_params=_cparams(("parallel", "arbitrary")),
        name="sample_scores",
    )(pt_flat, qcol, r3(q), r3(k_new), *([cache_kt] * PAGES_PER_STEP))


def _sample_attend_kernel(pages_ref, need_ref, vncol_ref, stat_ref, p_ref, *refs, page, pages_per_block, nb, steps):
    del pages_ref
    v_refs = refs[:PAGES_PER_STEP]
    o_ref, acc_ref = refs[PAGES_PER_STEP:]
    b = pl.program_id(0)
    st = pl.program_id(1)
    bps = PAGES_PER_STEP // pages_per_block

    @pl.when(st == 0)
    def _():
        acc_ref[...] = jnp.zeros_like(acc_ref)

    p_step = p_ref[0, 0]
    for bi in range(bps):
        @pl.when(need_ref[b * nb + st * bps + bi] != 0)
        def _(bi=bi):
            for h in range(N_HEADS):
                acc = acc_ref[h]
                for r in range(pages_per_block):
                    pg = bi * pages_per_block + r
                    acc = acc + v_refs[pg][0, h] * p_step[h:h + 1, pg * page:(pg + 1) * page]
                acc_ref[h] = acc

    @pl.when(st == steps - 1)
    def _():
        stat = stat_ref[0]
        for h in range(N_HEADS):
            tot = jnp.sum(acc_ref[h], axis=-1, keepdims=True)
            o_ref[0, h] = (tot + stat[h:h + 1, 1:2] * vncol_ref[0, h]) / stat[h:h + 1, 0:1]


def _fetched_pages(page_table, need, pages_per_block):
    db, n_pages = page_table.shape
    steps = n_pages // PAGES_PER_STEP
    needed = (jnp.repeat(need, pages_per_block, axis=1) != 0).reshape(db, steps, PAGES_PER_STEP)
    step_id = jnp.arange(steps, dtype=I32)[None, :, None]
    last = lax.cummax(jnp.where(needed, step_id, 0), axis=1)
    return jnp.take_along_axis(page_table.reshape(db, steps, PAGES_PER_STEP), last, axis=1).reshape(-1)


def _sample_attend(vncol, probs, stat, need, cache_vt, page_table):
    db, n_pages = page_table.shape
    page = cache_vt.shape[-1]
    ppb = MOBA_BLOCK // page
    steps = n_pages // PAGES_PER_STEP
    nb = n_pages // ppb
    span = PAGES_PER_STEP * page
    need2 = need.reshape(db, LANES)[:, :nb]
    col = lambda: pl.BlockSpec((1, N_HEADS, HEAD_DIM, 1), lambda b, s, *_: (b, 0, 0, 0))
    grid_spec = pltpu.PrefetchScalarGridSpec(
        num_scalar_prefetch=2,
        grid=(db, steps),
        in_specs=[col(),
                  pl.BlockSpec((1, N_HEADS, LANES), lambda b, s, *_: (b, 0, 0)),
                  pl.BlockSpec((1, 1, N_HEADS, span), lambda b, s, *_: (b, s, 0, 0))] + _page_specs(n_pages),
        out_specs=col(),
        scratch_shapes=[pltpu.VMEM((N_HEADS, HEAD_DIM, page), F32)],
    )
    out = pl.pallas_call(
        functools.partial(_sample_attend_kernel, page=page, pages_per_block=ppb, nb=nb, steps=steps),
        grid_spec=grid_spec,
        out_shape=jax.ShapeDtypeStruct((db, N_HEADS, HEAD_DIM, 1), F32),
        compiler_params=_cparams(("parallel", "arbitrary")),
        name="sample_attend",
    )(_fetched_pages(page_table, need2, ppb), need2.reshape(-1), vncol, stat, probs, *([cache_vt] * PAGES_PER_STEP))
    return out.reshape(db, N_HEADS * HEAD_DIM)


CONV_HALO = 32


def _ln_silu(y, g, b):
    mu = jnp.mean(y, axis=-1, keepdims=True)
    var = jnp.mean(jnp.square(y - mu), axis=-1, keepdims=True)
    n = (y - mu) * lax.rsqrt(var + EPS) * g + b
    return n * _sigmoid(n)


def _conv_prompt_kernel(zc_ref, zp_ref, dw_ref, db_ref, g_ref, b_ref, o_ref, win_ref, *, cw, tile):
    i = pl.program_id(1)
    prev = zp_ref[0, tile - CONV_HALO:tile, :]
    win_ref[0:CONV_HALO, :] = jnp.where(i > 0, prev, 0.0)
    win_ref[CONV_HALO:CONV_HALO + tile, :] = zc_ref[0]
    off = CONV_HALO - (cw - 1)
    y = jnp.broadcast_to(db_ref[...], (tile, db_ref.shape[-1]))
    for w in range(cw):
        y = y + win_ref[off + w:off + w + tile, :] * dw_ref[w:w + 1, :]
    o_ref[0] = _ln_silu(y, g_ref[...], b_ref[...])


def _conv_prompt(z, dw_w, dw_b, ln_g, ln_b, tile=256):
    b, s, c = z.shape
    cw = dw_w.shape[0]
    assert s % tile == 0 and cw - 1 <= CONV_HALO <= tile
    full = lambda a: pl.BlockSpec(a.shape, lambda bi, i: (0,) * a.ndim)
    return pl.pallas_call(
        functools.partial(_conv_prompt_kernel, cw=cw, tile=tile),
        grid=(b, s // tile),
        in_specs=[pl.BlockSpec((1, tile, c), lambda bi, i: (bi, i, 0)),
                  pl.BlockSpec((1, tile, c), lambda bi, i: (bi, jnp.maximum(i - 1, 0), 0)),
                  full(dw_w), full(dw_b), full(ln_g), full(ln_b)],
        out_specs=pl.BlockSpec((1, tile, c), lambda bi, i: (bi, i, 0)),
        out_shape=jax.ShapeDtypeStruct((b, s, c), F32),
        scratch_shapes=[pltpu.VMEM((CONV_HALO + tile, c), F32)],
        compiler_params=_cparams(("parallel", "arbitrary")),
        name="conv_prompt",
    )(z, z, dw_w, dw_b, ln_g, ln_b)


SAMPLE_CONV_ROWS = 8


def _conv_sample_kernel(st_ref, z_ref, dw_ref, db_ref, g_ref, b_ref, o_ref, ns_ref, *, cw):
    rows = []
    for r in range(SAMPLE_CONV_ROWS):
        y = jnp.sum(st_ref[r] * dw_ref[0:cw - 1, :], axis=0, keepdims=True)
        rows.append(y + z_ref[r:r + 1, :] * dw_ref[cw - 1:cw, :] + db_ref[...])
        ns_ref[r, 0:cw - 2, :] = st_ref[r, 1:cw - 1, :]
        ns_ref[r, cw - 2:cw - 1, :] = z_ref[r:r + 1, :]
    o_ref[...] = _ln_silu(jnp.concatenate(rows, axis=0), g_ref[...], b_ref[...])


def _conv_sample(state, z, dw_w, dw_b, ln_g, ln_b):
    db, sw, c = state.shape
    cw = dw_w.shape[0]
    r = SAMPLE_CONV_ROWS
    assert sw == cw - 1 and db % r == 0
    full = lambda a: pl.BlockSpec(a.shape, lambda i: (0,) * a.ndim)
    return pl.pallas_call(
        functools.partial(_conv_sample_kernel, cw=cw),
        grid=(db // r,),
        in_specs=[pl.BlockSpec((r, sw, c), lambda i: (i, 0, 0)), pl.BlockSpec((r, c), lambda i: (i, 0)),
                  full(dw_w), full(dw_b), full(ln_g), full(ln_b)],
        out_specs=[pl.BlockSpec((r, c), lambda i: (i, 0)), pl.BlockSpec((r, sw, c), lambda i: (i, 0, 0))],
        out_shape=[jax.ShapeDtypeStruct((db, c), F32), jax.ShapeDtypeStruct((db, sw, c), F32)],
        compiler_params=_cparams(("parallel",)),
        name="conv_sample",
    )(state, z, dw_w, dw_b, ln_g, ln_b)


def _outproj_kernel(x_ref, att_ref, conv_ref, sga_ref, sgb_ref, wao_ref, wco_ref, wo_ref, g2_ref, wq_ref,
                    x1_ref, hn_ref, pq_ref):
    merged = sga_ref[...] * _dot(att_ref[...], wao_ref[...]) + sgb_ref[...] * _dot(conv_ref[...], wco_ref[...])
    x1 = x_ref[...] + _dot(merged, wo_ref[...])
    x1_ref[...] = x1
    hn = _rms(x1, g2_ref[...])
    hn_ref[...] = hn
    pq_ref[...] = _dot(hn, wq_ref[...])


def _outproj(x2, att, conv, sga, sgb, wao, wco, wo, g2, wq, tm):
    t, d = x2.shape
    assert t % tm == 0
    row = lambda a: pl.BlockSpec((tm, a.shape[1]), lambda i: (i, 0))
    full = lambda a: pl.BlockSpec(a.shape, lambda i: (0,) * a.ndim)
    acts = (x2, att, conv, sga, sgb)
    wts = (wao, wco, wo, g2, wq)
    pq_w = wq.shape[1]
    return pl.pallas_call(
        _outproj_kernel,
        grid=(t // tm,),
        in_specs=[row(a) for a in acts] + [full(a) for a in wts],
        out_specs=[pl.BlockSpec((tm, d), lambda i: (i, 0)), pl.BlockSpec((tm, d), lambda i: (i, 0)),
                   pl.BlockSpec((tm, pq_w), lambda i: (i, 0))],
        out_shape=[jax.ShapeDtypeStruct((t, d), F32), jax.ShapeDtypeStruct((t, d), F32),
                   jax.ShapeDtypeStruct((t, pq_w), F32)],
        compiler_params=_cparams(("parallel",)),
        name="outproj",
    )(*acts, *wts)


PEER_TILE = 128
CAND_ROWS = PEER_TOPK + 8 * SUBLANES


def _peer_route_kernel(pq_ref, k12_ref, eidx_ref, pidx_ref, shift_ref, gate_ref, s_ref, c_ref, ci_ref, *, n_keys):
    kk = PEER_TOPK
    t = PEER_TILE
    nh = PEER_HEADS
    half = kk // 2
    assert kk == 16 and half == SUBLANES
    pair_span = n_keys * n_keys // 2
    key_id = lax.broadcasted_iota(I32, (n_keys, t), 0)
    kidx = lax.broadcasted_iota(I32, (kk, t), 0)
    for h in range(nh):
        qh = pq_ref[:, h * LANES:(h + 1) * LANES]
        s12 = _dot_nt(k12_ref[...], qh, HIGHEST)
        s_ref[2 * h] = s12[0:n_keys]
        s_ref[2 * h + 1] = s12[n_keys:2 * n_keys]

    def level1(it, carry):
        vals, idxs = carry
        nv, ni = [], []
        for c in range(2 * nh):
            s = s_ref[c]
            mx = jnp.max(s, axis=0, keepdims=True)
            pick = jnp.min(jnp.where(s == mx, key_id, n_keys), axis=0, keepdims=True)
            s_ref[c] = jnp.where(key_id == pick, -jnp.inf, s)
            nv.append(jnp.where(kidx == it, mx, vals[c]))
            ni.append(jnp.where(kidx == it, pick, idxs[c]))
        return tuple(nv), tuple(ni)

    zf = tuple(jnp.zeros((kk, t), F32) for _ in range(2 * nh))
    zi = tuple(jnp.zeros((kk, t), I32) for _ in range(2 * nh))
    vals, idxs = lax.fori_loop(0, kk, level1, (zf, zi))

    b16 = lax.broadcasted_iota(I32, (kk, t), 0)
    b8 = lax.broadcasted_iota(I32, (half, t), 0)
    flat_parts = [b16, kk + b8]
    for a in range(2, half):
        flat_parts.append(a * kk + b8)
    flat_parts.append((half + b8) * kk)
    flat = jnp.concatenate(flat_parts, axis=0)
    for h in range(nh):
        v1, v2, i1, i2 = vals[2 * h], vals[2 * h + 1], idxs[2 * h], idxs[2 * h + 1]
        c_parts = [v1[0:1] + v2, v1[1:2] + v2[0:half]]
        i_parts = [i1[0:1] * n_keys + i2, i1[1:2] * n_keys + i2[0:half]]
        for a in range(2, half):
            c_parts.append(jnp.where(b8 < kk // (a + 1), v1[a:a + 1] + v2[0:half], -jnp.inf))
            i_parts.append(i1[a:a + 1] * n_keys + i2[0:half])
        c_parts.append(v1[half:kk] + v2[0:1])
        i_parts.append(i1[half:kk] * n_keys + i2[0:1])
        c_ref[h] = jnp.concatenate(c_parts, axis=0)
        ci_ref[h] = jnp.concatenate(i_parts, axis=0)

    def level2(it, carry):
        vals2, exps = carry
        nv, ne = [], []
        for h in range(nh):
            c = c_ref[h]
            mx = jnp.max(c, axis=0, keepdims=True)
            pick = jnp.min(jnp.where(c == mx, flat, kk * kk), axis=0, keepdims=True)
            hit = flat == pick
            e = jnp.max(jnp.where(hit, ci_ref[h], -1), axis=0, keepdims=True)
            c_ref[h] = jnp.where(hit, -jnp.inf, c)
            nv.append(jnp.where(kidx == it, mx, vals2[h]))
            ne.append(jnp.where(kidx == it, e, exps[h]))
        return tuple(nv), tuple(ne)

    sc, eidx = lax.fori_loop(0, kk, level2, (zf[:nh], zi[:nh]))
    for h in range(nh):
        ex = jnp.exp(sc[h] - jnp.max(sc[h], axis=0, keepdims=True))
        rows = slice(h * kk, (h + 1) * kk)
        gate_ref[0, rows, :] = ex / jnp.sum(ex, axis=0, keepdims=True)
        eidx_ref[0, rows, :] = eidx[h]
        upper = (eidx[h] >= pair_span).astype(I32)
        pidx_ref[0, rows, :] = eidx[h] - upper * pair_span
        shift_ref[0, rows, :] = (1 - upper) * 16


def _peer_route(pq, k1ext, k2ext):
    t, w = pq.shape
    n_keys = k1ext.shape[0]
    slots = PEER_HEADS * PEER_TOPK
    assert t % PEER_TILE == 0 and w == PEER_HEADS * LANES and k1ext.shape[1] == LANES
    k12 = jnp.concatenate([k1ext, k2ext], axis=0)
    nt = t // PEER_TILE
    full = lambda a: pl.BlockSpec(a.shape, lambda i: (0,) * a.ndim)
    ospec = pl.BlockSpec((1, slots, PEER_TILE), lambda i: (i, 0, 0))
    ishape = jax.ShapeDtypeStruct((nt, slots, PEER_TILE), I32)
    return pl.pallas_call(
        functools.partial(_peer_route_kernel, n_keys=n_keys),
        grid=(nt,),
        in_specs=[pl.BlockSpec((PEER_TILE, w), lambda i: (i, 0)), full(k12)],
        out_specs=[ospec, ospec, ospec, ospec],
        out_shape=[ishape, ishape, ishape, jax.ShapeDtypeStruct((nt, slots, PEER_TILE), F32)],
        scratch_shapes=[pltpu.VMEM((2 * PEER_HEADS, n_keys, PEER_TILE), F32),
                        pltpu.VMEM((PEER_HEADS, CAND_ROWS, PEER_TILE), F32),
                        pltpu.VMEM((PEER_HEADS, CAND_ROWS, PEER_TILE), I32)],
        compiler_params=_cparams(("parallel",)),
        name="peer_route",
    )(pq, k12)


HI_MASK = np.uint32(0xFFFF0000)
HALF_SUB = SUBLANES // 2


def _as_f32(bits):
    return lax.bitcast_convert_type(bits, F32)


def _tile_copies(hbm_refs, smem_refs, sems, tile, slot, n):
    dst = pl.ds(pl.multiple_of(slot * n, n), n)
    return [pltpu.make_async_copy(h.at[tile], s.at[dst], sems.at[k, slot])
            for k, (h, s) in enumerate(zip(hbm_refs, smem_refs))]


def _stream_tiles(hbm_refs, smem_refs, sems, n_tiles, n):
    i = pl.program_id(0)
    slot = i % 2

    @pl.when(i == 0)
    def _():
        for c in _tile_copies(hbm_refs, smem_refs, sems, 0, 0, n):
            c.start()

    @pl.when(i + 1 < n_tiles)
    def _():
        for c in _tile_copies(hbm_refs, smem_refs, sems, i + 1, 1 - slot, n):
            c.start()

    for c in _tile_copies(hbm_refs, smem_refs, sems, i, slot, n):
        c.wait()
    return slot


SCORE_CHUNK_GROUPS = 8


def _peer_score_kernel(pidx_hbm, shift_hbm, x_ref, gate_ref, tab_ref, act_ref, pidx_sm, shift_sm, dots_ref, grp_ref, sems,
                       *, n_tiles, slots):
    tt = PEER_TILE
    n = slots * tt
    slot = _stream_tiles((pidx_hbm, shift_hbm), (pidx_sm, shift_sm), sems, n_tiles, n)
    lane = lax.broadcasted_iota(I32, (slots, tt), 1)
    dots_ref[...] = jnp.zeros_like(dots_ref)
    chunk_rows = SCORE_CHUNK_GROUPS * SUBLANES
    span = chunk_rows * tt
    view_len = 2 * n - (chunk_rows - 1) * tt

    lane_c = lax.broadcasted_iota(I32, (chunk_rows, tt), 1)
    grp_ref[...] = jnp.zeros_like(grp_ref)

    def finish(tok, buf, c):
        rows = pl.ds(pl.multiple_of(c * chunk_rows, chunk_rows), chunk_rows)
        part = grp_ref[buf, pl.ds(c * SCORE_CHUNK_GROUPS, SCORE_CHUNK_GROUPS)]
        col = jnp.sum(part.reshape(chunk_rows, LANES), axis=-1, keepdims=True)
        dots_ref[rows, :] = jnp.where(lane_c == tok, col, dots_ref[rows, :])

    def token(t, carry):
        base = slot * n + t
        x = x_ref[t]
        cur = t % 2

        def chunk(c, inner):
            finish(t - 1, 1 - cur, c)
            cbase = base + c * span
            for g in range(SCORE_CHUNK_GROUPS):
                prods = []
                for r in range(SUBLANES):
                    view = pl.ds((g * SUBLANES + r) * tt, view_len)
                    w = tab_ref[pidx_sm.at[view][cbase]]
                    prods.append(_as_f32((w << shift_sm.at[view][cbase].astype(U32)) & HI_MASK) * x)
                grp_ref[cur, c * SCORE_CHUNK_GROUPS + g] = _sublane_sums(prods)
            return inner

        lax.fori_loop(0, slots // chunk_rows, chunk, 0)
        return carry

    lax.fori_loop(0, tt, token, 0)
    for c in range(slots // chunk_rows):
        finish(tt - 1, (tt - 1) % 2, c)
    act_ref[0] = _gelu_erf(dots_ref[...]) * gate_ref[0]


def _peer_scores(x3, pidx, shift, gate, tab):
    nt, slots, tt = gate.shape
    n = slots * tt
    flat = lambda a: a.reshape(nt, n)
    kern = functools.partial(_peer_score_kernel, n_tiles=nt, slots=slots)
    return pl.pallas_call(
        kern,
        grid=(nt,),
        in_specs=[pl.BlockSpec(memory_space=pl.ANY), pl.BlockSpec(memory_space=pl.ANY),
                  pl.BlockSpec((tt, SUBLANES, LANES), lambda i: (i, 0, 0)),
                  pl.BlockSpec((1, slots, tt), lambda i: (i, 0, 0)),
                  pl.BlockSpec(memory_space=pltpu.VMEM)],
        out_specs=pl.BlockSpec((1, slots, tt), lambda i: (i, 0, 0)),
        out_shape=jax.ShapeDtypeStruct((nt, slots, tt), F32),
        scratch_shapes=[pltpu.SMEM((2 * n,), I32), pltpu.SMEM((2 * n,), I32),
                        pltpu.VMEM((slots, tt), F32), pltpu.VMEM((2, slots // SUBLANES, SUBLANES, LANES), F32),
                        pltpu.SemaphoreType.DMA((2, 2))],
        compiler_params=_cparams(("arbitrary",)),
        name="peer_scores",
    )(flat(pidx), flat(shift), x3, gate, tab)


MIX_ACCS = 4


MIX_CHUNK = 64


def _peer_mix_kernel(eidx_hbm, act_hbm, tab_ref, o_ref, eidx_sm, act_sm, sems, *, n_tiles, slots):
    tt = PEER_TILE
    n = slots * tt
    slot = _stream_tiles((eidx_hbm, act_hbm), (eidx_sm, act_sm), sems, n_tiles, n)
    span = MIX_CHUNK * tt
    view_len = 2 * n - (MIX_CHUNK - 1) * tt

    def token(t, carry):
        base = slot * n + t

        def chunk(c, accs):
            lo, hi = list(accs[0]), list(accs[1])
            cbase = base + c * span
            for r in range(MIX_CHUNK):
                view = pl.ds(r * tt, view_len)
                w = tab_ref[eidx_sm.at[view][cbase]]
                a = act_sm.at[view][cbase]
                lo[r % MIX_ACCS] = lo[r % MIX_ACCS] + a * _as_f32(w << 16)
                hi[r % MIX_ACCS] = hi[r % MIX_ACCS] + a * _as_f32(w & HI_MASK)
            return tuple(lo), tuple(hi)

        zero = tuple(jnp.zeros((HALF_SUB, LANES), F32) for _ in range(MIX_ACCS))
        lo, hi = lax.fori_loop(0, slots // MIX_CHUNK, chunk, (zero, zero))
        o_ref[t, 0:HALF_SUB, :] = (lo[0] + lo[1]) + (lo[2] + lo[3])
        o_ref[t, HALF_SUB:SUBLANES, :] = (hi[0] + hi[1]) + (hi[2] + hi[3])
        return carry

    lax.fori_loop(0, tt, token, 0)


def _peer_mix(eidx, act, tab):
    nt, slots, tt = act.shape
    n = slots * tt
    flat = lambda a: a.reshape(nt, n)
    kern = functools.partial(_peer_mix_kernel, n_tiles=nt, slots=slots)
    return pl.pallas_call(
        kern,
        grid=(nt,),
        in_specs=[pl.BlockSpec(memory_space=pl.ANY), pl.BlockSpec(memory_space=pl.ANY),
                  pl.BlockSpec(memory_space=pltpu.VMEM)],
        out_specs=pl.BlockSpec((tt, SUBLANES, LANES), lambda i: (i, 0, 0)),
        out_shape=jax.ShapeDtypeStruct((nt * tt, SUBLANES, LANES), F32),
        scratch_shapes=[pltpu.SMEM((2 * n,), I32), pltpu.SMEM((2 * n,), F32), pltpu.SemaphoreType.DMA((2, 2))],
        compiler_params=_cparams(("arbitrary",)),
        name="peer_mix",
    )(flat(eidx), flat(act), tab)


def _bf16_bits(tab):
    return lax.bitcast_convert_type(tab.astype(BF16), jnp.uint16).astype(U32)


def _pack_pairs(tab):
    n, d = tab.shape
    assert n % 2 == 0 and d == SUBLANES * LANES
    bits = _bf16_bits(tab)
    return (bits[:n // 2] | (bits[n // 2:] << 16)).reshape(n // 2, SUBLANES, LANES)


def _pack_halves(tab):
    n, d = tab.shape
    assert d == SUBLANES * LANES
    bits = _bf16_bits(tab)
    return (bits[:, :d // 2] | (bits[:, d // 2:] << 16)).reshape(n, HALF_SUB, LANES)


def _peer(hn, pq, k1ext, k2ext, u_pack, v_pack):
    t, d = hn.shape
    eidx, pidx, shift, gate = _peer_route(pq, k1ext, k2ext)
    act = _peer_scores(hn.reshape(t, SUBLANES, LANES), pidx, shift, gate, u_pack)
    return _peer_mix(eidx, act, v_pack).reshape(t, d)


def _ple_kernel(x1_ref, po_ref, pe_ref, g3_ref, wg_ref, wp_ref, gf_ref, y_ref, *, final):
    x2 = x1_ref[...] + po_ref[...]
    gate = _sigmoid(_dot(_rms(x2, g3_ref[...]), wg_ref[...]))
    x3 = x2 + gate * _dot(pe_ref[...], wp_ref[...])
    y_ref[...] = _rms(x3, gf_ref[...]) if final else x3


def _ple(x1, po, pe, g3, wg, wp, gf, tm, final):
    t, d = x1.shape
    assert t % tm == 0
    row = lambda a: pl.BlockSpec((tm, a.shape[1]), lambda i: (i, 0))
    full = lambda a: pl.BlockSpec(a.shape, lambda i: (0,) * a.ndim)
    return pl.pallas_call(
        functools.partial(_ple_kernel, final=final),
        grid=(t // tm,),
        in_specs=[row(x1), row(po), row(pe), full(g3), full(wg), full(wp), full(gf)],
        out_specs=pl.BlockSpec((tm, d), lambda i: (i, 0)),
        out_shape=jax.ShapeDtypeStruct((t, d), F32),
        compiler_params=_cparams(("parallel",)),
        name="ple_final",
    )(x1, po, pe, g3, wg, wp, gf)


def kernel(x_prompt, x_sample, cache_k, cache_v, state_conv, page_table, p_prompt, p_sample, norm1_g, w_in, dw_w, dw_b, cln_g, cln_b, w_att_out, w_conv_out, w_o, norm2_g, peer_wq, peer_k1, peer_k2, peer_u, peer_v, norm3_g, w_ple, w_ple_gate, final_g):
    depth = w_in.shape[0]
    b, s, d = x_prompt.shape
    db, ds, _ = x_sample.shape
    page = cache_k.shape[2]
    n_pages = page_table.shape[1]
    att_w = N_HEADS * HEAD_DIM
    assert ds == 1 and (n_pages * page) % MOBA_BLOCK == 0
    row = lambda a: a.reshape(1, -1)
    pt_flat = page_table.reshape(-1).astype(I32)
    gf = row(final_g)
    assert 2 * peer_k1.shape[-1] == LANES

    xp = x_prompt.reshape(b * s, d)
    xs = x_sample.reshape(db, d)
    outs = [[] for _ in range(6)]
    for li in range(depth):
        last = li == depth - 1
        bf = lambda a: a[li].astype(BF16)
        w_in_b, wao, wco, wo, wq, wpg, wpl = (bf(a) for a in (w_in, w_att_out, w_conv_out, w_o, peer_wq, w_ple_gate, w_ple))
        zeros = jnp.zeros_like(peer_k1[li])
        k1ext = jnp.concatenate([peer_k1[li], zeros], axis=1)
        k2ext = jnp.concatenate([zeros, peer_k2[li]], axis=1)
        u_pack = _pack_pairs(peer_u[li])
        v_pack = _pack_halves(peer_v[li])
        conv_w = (dw_w[li], row(dw_b[li]), row(cln_g[li]), row(cln_b[li]))
        g1, g2, g3 = row(norm1_g[li]), row(norm2_g[li]), row(norm3_g[li])
        ckt = jnp.transpose(cache_k[li], (0, 2, 3, 1))
        cvt = jnp.transpose(cache_v[li], (0, 2, 3, 1))

        qs, ks, vs, zs, sgas, sgbs = _inproj(xs, g1, w_in_b, db)
        col = lambda a: a.reshape(db, N_HEADS, HEAD_DIM, 1)
        probs, stat, need = _sample_scores(qs, ks, col(qs), ckt, pt_flat, n_pages)
        atts = _sample_attend(col(vs), probs, stat, need, cvt, page_table.astype(I32))
        convs, new_state = _conv_sample(state_conv[li], zs, *conv_w)
        x1s, hns, pqs = _outproj(xs, atts, convs, sgas, sgbs, wao, wco, wo, g2, wq, db)
        pos = _peer(hns, pqs, k1ext, k2ext, u_pack, v_pack)
        xs = _ple(x1s, pos, p_sample[li].reshape(db, -1), g3, wpg, wpl, gf, db, last)
        outs[3].append(ks.reshape(db, ds, N_HEADS, HEAD_DIM))
        outs[4].append(vs.reshape(db, ds, N_HEADS, HEAD_DIM))
        outs[5].append(new_state)

        q, k, v, z, sga, sgb, kt, vt = _inproj(xp, g1, w_in_b, 256, seq_len=s)
        att = _moba_prompt(q.reshape(b, s, att_w), k.reshape(b, s, att_w), v.reshape(b, s, att_w))
        z3 = z.reshape(b, s, -1)
        conv = _conv_prompt(z3, *conv_w)
        x1, hn, pq = _outproj(xp, att.reshape(b * s, att_w), conv.reshape(b * s, -1), sga, sgb, wao, wco, wo, g2, wq, 256)
        po = _peer(hn, pq, k1ext, k2ext, u_pack, v_pack)
        xp = _ple(x1, po, p_prompt[li].reshape(b * s, -1), g3, wpg, wpl, gf, 256, last)
        outs[0].append(jnp.transpose(kt, (0, 3, 1, 2)))
        outs[1].append(jnp.transpose(vt, (0, 3, 1, 2)))
        outs[2].append(z3[:, s - (dw_w.shape[1] - 1):, :])

    return (xp.reshape(b, s, d), xs.reshape(db, ds, d)) + tuple(jnp.stack(o) for o in outs)
```

```python
import functools

import numpy as np
import jax
import jax.numpy as jnp
from jax import lax
from jax.experimental import pallas as pl
from jax.experimental.pallas import tpu as pltpu

F32 = jnp.float32
BF16 = jnp.bfloat16
I32 = jnp.int32
U32 = jnp.uint32

N_HEADS = 8
HEAD_DIM = 64
MOBA_BLOCK = 256
MOBA_TOPK = 3
PEER_HEADS = 8
PEER_TOPK = 16
EPS = 1e-6
NEG = -1e30
ATT_SCALE = HEAD_DIM ** -0.5

LANES = 128
SUBLANES = 8
VMEM_LIMIT_BYTES = 56 * 1024 * 1024

HIGHEST = lax.Precision.HIGHEST


def _cparams(sem):
    return pltpu.CompilerParams(dimension_semantics=sem, vmem_limit_bytes=VMEM_LIMIT_BYTES)


def _rms(x, g):
    return x * lax.rsqrt(jnp.mean(x * x, axis=-1, keepdims=True) + EPS) * g


def _sigmoid(x):
    return 1.0 / (1.0 + jnp.exp(-x))


def _dot(a, b):
    return jnp.dot(a.astype(BF16), b.astype(BF16), preferred_element_type=F32)


def _dot_nt(a, b, precision=None):
    return lax.dot_general(a, b, (((1,), (1,)), ((), ())), preferred_element_type=F32, precision=precision)


def _gelu_erf(x):
    return 0.5 * x * (1.0 + lax.erf(x * np.float32(1.0 / np.sqrt(2.0))))


def _inproj_kernel(x_ref, g_ref, w_ref, q_ref, k_ref, v_ref, z_ref, sga_ref, sgb_ref, *t_refs, att_w, conv_ch, d_model):
    h = _rms(x_ref[...], g_ref[...]).astype(BF16)

    def proj(lo, width):
        return jnp.dot(h, w_ref[:, lo:lo + width], preferred_element_type=F32)

    q_ref[...] = proj(0, att_w)
    k = proj(att_w, att_w)
    v = proj(2 * att_w, att_w)
    k_ref[...] = k
    v_ref[...] = v
    if t_refs:
        kt_ref, vt_ref = t_refs
        kt_ref[0] = k.T.reshape(N_HEADS, HEAD_DIM, k.shape[0])
        vt_ref[0] = v.T.reshape(N_HEADS, HEAD_DIM, v.shape[0])
    base = 3 * att_w
    z_ref[...] = proj(base, conv_ch) * _sigmoid(proj(base + conv_ch, conv_ch))
    base += 2 * conv_ch
    sga_ref[...] = _sigmoid(proj(base, d_model))
    sgb_ref[...] = _sigmoid(proj(base + d_model, d_model))


def _inproj(x2, g, w_bf, tm, seq_len=None):
    t, d = x2.shape
    att_w = N_HEADS * HEAD_DIM
    conv_ch = d // 2
    assert w_bf.shape[1] == 3 * att_w + 2 * conv_ch + 2 * d and t % tm == 0
    row = lambda w: pl.BlockSpec((tm, w), lambda i: (i, 0))
    full = lambda a: pl.BlockSpec(a.shape, lambda i: (0,) * a.ndim)
    out_w = (att_w, att_w, att_w, conv_ch, d, d)
    out_specs = [row(w) for w in out_w]
    out_shape = [jax.ShapeDtypeStruct((t, w), F32) for w in out_w]
    if seq_len is not None:
        assert seq_len % tm == 0 and t % seq_len == 0 and tm % LANES == 0
        tiles = seq_len // tm
        t_spec = pl.BlockSpec((1, N_HEADS, HEAD_DIM, tm), lambda i: (i // tiles, 0, 0, i % tiles))
        t_shape = jax.ShapeDtypeStruct((t // seq_len, N_HEADS, HEAD_DIM, seq_len), F32)
        out_specs += [t_spec, t_spec]
        out_shape += [t_shape, t_shape]
    return pl.pallas_call(
        functools.partial(_inproj_kernel, att_w=att_w, conv_ch=conv_ch, d_model=d),
        grid=(t // tm,),
        in_specs=[row(d), full(g), full(w_bf)],
        out_specs=out_specs,
        out_shape=out_shape,
        compiler_params=_cparams(("parallel",)),
        name="inproj",
    )(x2, g, w_bf)


def _moba_prompt_kernel(q_ref, k_ref, v_ref, o_ref, kmean_ref, sel_ref, *, nb, rq):
    blk = MOBA_BLOCK
    i = pl.program_id(2)
    own = (i * rq) // blk
    off = (i * rq) % blk

    @pl.when(i == 0)
    def _():
        for j in range(nb):
            kmean_ref[j:j + 1, :] = jnp.sum(k_ref[0, j * blk:(j + 1) * blk, :], axis=0, keepdims=True) * (1.0 / blk)

    lane = lax.broadcasted_iota(I32, (1, LANES), 1)
    row_i = lax.broadcasted_iota(I32, (rq, blk), 0) + off
    col_i = lax.broadcasted_iota(I32, (rq, blk), 1)
    causal = col_i <= row_i
    kmean = kmean_ref[...]
    bidx = lax.broadcasted_iota(I32, (nb, rq), 0)
    expand = (lax.broadcasted_iota(I32, (LANES, 2 * nb * LANES), 1) // LANES
              == lax.broadcasted_iota(I32, (LANES, 2 * nb * LANES), 0)).astype(BF16)
    head_a = lane < HEAD_DIM
    q_all = q_ref[0]
    kmean2 = jnp.concatenate([jnp.where(head_a, kmean, 0.0), jnp.where(head_a, 0.0, kmean)], axis=0)
    gates2 = _dot_nt(kmean2, q_all, HIGHEST)
    sel_parts = []
    for hs in range(2):
        gates = jnp.where(bidx < own, gates2[hs * nb:(hs + 1) * nb], NEG)
        rank = jnp.zeros((nb, rq), F32)
        for m in range(nb):
            gm = gates[m:m + 1, :]
            ahead = ((bidx > m) & (gm >= gates)) | ((bidx < m) & (gm > gates))
            rank = rank + ahead.astype(F32)
        sel_parts.append(jnp.where((rank < MOBA_TOPK) & (bidx < own), 1.0, 0.0))
    sel_q = jnp.concatenate(sel_parts + [jnp.zeros((LANES - 2 * nb, rq), F32)], axis=0).T
    sel_all = _dot(sel_q, expand)
    for hs in range(2):
        for j in range(nb):
            slab = hs * nb + j
            sel_ref[hs, j] = sel_all[:, slab * LANES:(slab + 1) * LANES]
    qbs = [jnp.where(head_a, q_all, 0.0).astype(BF16), jnp.where(head_a, 0.0, q_all).astype(BF16)]

    def step(s, mask, vb, carry):
        m_run, l_run, acc = carry
        s = jnp.where(mask, s * ATT_SCALE, NEG)
        m_new = jnp.maximum(m_run, jnp.max(s, axis=-1, keepdims=True))
        p = jnp.where(mask, jnp.exp(s - m_new), 0.0)
        alpha = jnp.exp(m_run - m_new)
        l_new = alpha * l_run + jnp.sum(p, axis=-1, keepdims=True)
        acc_new = alpha * acc + jnp.dot(p.astype(BF16), vb, preferred_element_type=F32)
        return m_new, l_new, acc_new

    def both_heads(j, n_blocks, masks, carries):
        start = pl.multiple_of(j * blk, blk)
        kb = k_ref[0, pl.ds(start, n_blocks * blk), :].astype(BF16)
        vb = v_ref[0, pl.ds(start, n_blocks * blk), :].astype(BF16)
        return tuple(step(_dot_nt(qbs[hs], kb), masks[hs], vb, carries[hs]) for hs in range(2))

    def past_mask(hs, j):
        sj = sel_ref[hs, j]
        return jnp.concatenate([sj, sj], axis=1) > 0.5

    def pair_body(jj, carries):
        masks = [jnp.concatenate([past_mask(hs, 2 * jj), past_mask(hs, 2 * jj + 1)], axis=1) for hs in range(2)]
        return both_heads(2 * jj, 2, masks, carries)

    def single_body(j, carries):
        return both_heads(j, 1, [past_mask(hs, j) for hs in range(2)], carries)

    init = (jnp.full((rq, 1), NEG, F32), jnp.zeros((rq, 1), F32), jnp.zeros((rq, LANES), F32))
    pairs = own // 2
    carries = lax.fori_loop(0, pairs, pair_body, (init, init))
    carries = lax.fori_loop(2 * pairs, own, single_body, carries)
    (_, l_a, acc_a), (_, l_b, acc_b) = both_heads(own, 1, (causal, causal), carries)
    o_ref[0] = jnp.where(lane < HEAD_DIM, acc_a / l_a, acc_b / l_b)


MOBA_Q_ROWS = 256


def _moba_prompt(q, k, v):
    b, s, w = q.shape
    blk = MOBA_BLOCK
    rq = MOBA_Q_ROWS
    nb = s // blk
    assert s % blk == 0 and blk % rq == 0 and rq % LANES == 0 and MOBA_TOPK <= nb and 2 * nb <= LANES
    assert w % LANES == 0 and LANES == 2 * HEAD_DIM and blk == 2 * LANES
    kv_spec = pl.BlockSpec((1, s, LANES), lambda bi, hp, i: (bi, 0, hp))
    q_spec = pl.BlockSpec((1, rq, LANES), lambda bi, hp, i: (bi, i, hp))
    return pl.pallas_call(
        functools.partial(_moba_prompt_kernel, nb=nb, rq=rq),
        grid=(b, w // LANES, s // rq),
        in_specs=[q_spec, kv_spec, kv_spec],
        out_specs=q_spec,
        out_shape=jax.ShapeDtypeStruct((b, s, w), F32),
        scratch_shapes=[pltpu.VMEM((nb, LANES), F32), pltpu.VMEM((2, nb, rq, LANES), F32)],
        compiler_params=_cparams(("parallel", "parallel", "arbitrary")),
        name="moba_prompt",
    )(q, k, v)


PAGES_PER_STEP = 16


def _sublane_sums(vs):
    sub = lax.broadcasted_iota(I32, (SUBLANES, LANES), 0)
    cur = [vs[_BUTTERFLY_ORDER[p]] for p in range(SUBLANES)]
    for shift, keep in ((1, sub % 2 == 0), (2, sub % 4 < 2), (4, sub < 4)):
        nxt = []
        for a, b in zip(cur[0::2], cur[1::2]):
            nxt.append(jnp.where(keep, a, b) + pltpu.roll(jnp.where(keep, b, a), shift, 0))
        cur = nxt
    return cur[0]


def _butterfly_order():
    ids = [np.full((SUBLANES,), i) for i in range(SUBLANES)]
    sub = np.arange(SUBLANES)
    for shift, keep in ((1, sub % 2 == 0), (2, sub % 4 < 2), (4, sub < 4)):
        nxt = []
        for a, b in zip(ids[0::2], ids[1::2]):
            first = np.where(keep, a, b)
            assert (first == np.roll(np.where(keep, b, a), shift)).all()
            nxt.append(first)
        ids = nxt
    order = np.empty(SUBLANES, np.int64)
    order[ids[0]] = np.arange(SUBLANES)
    return [int(v) for v in order]


_BUTTERFLY_ORDER = _butterfly_order()


def _head_rows(row):
    w = row.shape[-1]
    lane_head = lax.broadcasted_iota(I32, (N_HEADS, w), 1) // HEAD_DIM
    sub = lax.broadcasted_iota(I32, (N_HEADS, w), 0)
    return jnp.where(lane_head == sub, jnp.broadcast_to(row, (N_HEADS, w)), 0.0)


def _sample_scores_kernel(pt_ref, qcol_ref, q_ref, kn_ref, *refs, page, pages_per_block, nb, steps):
    del pt_ref
    k_refs = refs[:PAGES_PER_STEP]
    p_ref, stat_ref, need_ref, qb_ref, gacc_ref = refs[PAGES_PER_STEP:]
    st = pl.program_id(1)
    bps = PAGES_PER_STEP // pages_per_block
    span = PAGES_PER_STEP * page

    @pl.when(st == 0)
    def _():
        qb_ref[...] = jnp.broadcast_to(qcol_ref[0], qb_ref.shape)
        gacc_ref[...] = jnp.zeros_like(gacc_ref)

    lane = lax.broadcasted_iota(I32, (N_HEADS, LANES), 1)
    raws = []
    for p in range(PAGES_PER_STEP):
        per_head = []
        for h in range(N_HEADS):
            prod = k_refs[p][0, h] * qb_ref[h]
            part = prod[0:SUBLANES]
            for c in range(1, HEAD_DIM // SUBLANES):
                part = part + prod[c * SUBLANES:(c + 1) * SUBLANES]
            per_head.append(part)
        raw = _sublane_sums(per_head)
        raws.append(raw)
        p_ref[0, st, :, p * page:(p + 1) * page] = raw * ATT_SCALE
    gacc = gacc_ref[...]
    for bi in range(bps):
        tot = raws[bi * pages_per_block]
        for r in range(1, pages_per_block):
            tot = tot + raws[bi * pages_per_block + r]
        gacc = jnp.where(lane == st * bps + bi, jnp.sum(tot, axis=-1, keepdims=True), gacc)
    gacc_ref[...] = gacc

    @pl.when(st == steps - 1)
    def _():
        gate = jnp.where(lane < nb, gacc * (1.0 / MOBA_BLOCK), -jnp.inf)
        sel = jnp.zeros((N_HEADS, LANES), F32)
        for _ in range(MOBA_TOPK):
            mx = jnp.max(gate, axis=-1, keepdims=True)
            pick = jnp.min(jnp.where(gate == mx, lane, LANES), axis=-1, keepdims=True)
            hit = lane == pick
            sel = jnp.where(hit, 1.0, sel)
            gate = jnp.where(hit, -jnp.inf, gate)
        n_keys = nb * MOBA_BLOCK
        expand = (lax.broadcasted_iota(I32, (LANES, n_keys), 1) // MOBA_BLOCK
                  == lax.broadcasted_iota(I32, (LANES, n_keys), 0))
        mask = _dot(sel, expand.astype(F32)) > 0.5
        masks = [mask[:, t * span:(t + 1) * span] for t in range(steps)]
        s_new = jnp.sum(_head_rows(q_ref[0]) * kn_ref[0], axis=-1, keepdims=True) * ATT_SCALE
        m = s_new
        for t in range(steps):
            m = jnp.maximum(m, jnp.max(jnp.where(masks[t], p_ref[0, t], NEG), axis=-1, keepdims=True))
        pn = jnp.exp(s_new - m)
        l = pn
        for t in range(steps):
            p = jnp.where(masks[t], jnp.exp(p_ref[0, t] - m), 0.0)
            l = l + jnp.sum(p, axis=-1, keepdims=True)
            p_ref[0, t] = p
        stat_ref[0] = jnp.where(lane == 0, l, jnp.where(lane == 1, pn, 0.0))
        need_ref[0] = jnp.max(sel, axis=0, keepdims=True).astype(I32)


def _page_specs(n_pages):
    def make(p):
        return pl.BlockSpec((1, N_HEADS, HEAD_DIM, LANES),
                            lambda b, s, pages, *_: (pages[b * n_pages + s * PAGES_PER_STEP + p], 0, 0, 0))
    return [make(p) for p in range(PAGES_PER_STEP)]


def _sample_scores(q, k_new, qcol, cache_kt, pt_flat, n_pages):
    db, w = q.shape
    page = cache_kt.shape[-1]
    ppb = MOBA_BLOCK // page
    steps = n_pages // PAGES_PER_STEP
    nb = n_pages // ppb
    span = PAGES_PER_STEP * page
    assert page == LANES and n_pages % PAGES_PER_STEP == 0 and PAGES_PER_STEP % ppb == 0 and MOBA_TOPK <= nb <= LANES
    one = lambda: pl.BlockSpec((1, 1, w), lambda b, s, pt: (b, 0, 0))
    grid_spec = pltpu.PrefetchScalarGridSpec(
        num_scalar_prefetch=1,
        grid=(db, steps),
        in_specs=[pl.BlockSpec((1, N_HEADS, HEAD_DIM, 1), lambda b, s, pt: (b, 0, 0, 0)), one(), one()]
        + _page_specs(n_pages),
        out_specs=[pl.BlockSpec((1, steps, N_HEADS, span), lambda b, s, pt: (b, 0, 0, 0)),
                   pl.BlockSpec((1, N_HEADS, LANES), lambda b, s, pt: (b, 0, 0)),
                   pl.BlockSpec((1, 1, LANES), lambda b, s, pt: (b, 0, 0))],
        scratch_shapes=[pltpu.VMEM((N_HEADS, HEAD_DIM, page), F32), pltpu.VMEM((N_HEADS, LANES), F32)],
    )
    r3 = lambda a: a.reshape(db, 1, w)
    return pl.pallas_call(
        functools.partial(_sample_scores_kernel, page=page, pages_per_block=ppb, nb=nb, steps=steps),
        grid_spec=grid_spec,
        out_shape=[jax.ShapeDtypeStruct((db, steps, N_HEADS, span), F32),
                   jax.ShapeDtypeStruct((db, N_HEADS, LANES), F32),
                   jax.ShapeDtypeStruct((db, 1, LANES), I32)],
        compiler_params=_cparams(("parallel", "arbitrary")),
        name="sample_scores",
    )(pt_flat, qcol, r3(q), r3(k_new), *([cache_kt] * PAGES_PER_STEP))


def _sample_attend_kernel(pages_ref, need_ref, vncol_ref, stat_ref, p_ref, *refs, page, pages_per_block, nb, steps):
    del pages_ref
    v_refs = refs[:PAGES_PER_STEP]
    o_ref, acc_ref = refs[PAGES_PER_STEP:]
    b = pl.program_id(0)
    st = pl.program_id(1)
    bps = PAGES_PER_STEP // pages_per_block

    @pl.when(st == 0)
    def _():
        acc_ref[...] = jnp.zeros_like(acc_ref)

    p_step = p_ref[0, 0]
    for bi in range(bps):
        @pl.when(need_ref[b * nb + st * bps + bi] != 0)
        def _(bi=bi):
            for h in range(N_HEADS):
                acc = acc_ref[h]
                for r in range(pages_per_block):
                    pg = bi * pages_per_block + r
                    acc = acc + v_refs[pg][0, h] * p_step[h:h + 1, pg * page:(pg + 1) * page]
                acc_ref[h] = acc

    @pl.when(st == steps - 1)
    def _():
        stat = stat_ref[0]
        for h in range(N_HEADS):
            tot = jnp.sum(acc_ref[h], axis=-1, keepdims=True)
            o_ref[0, h] = (tot + stat[h:h + 1, 1:2] * vncol_ref[0, h]) / stat[h:h + 1, 0:1]


def _fetched_pages(page_table, need, pages_per_block):
    db, n_pages = page_table.shape
    steps = n_pages // PAGES_PER_STEP
    needed = (jnp.repeat(need, pages_per_block, axis=1) != 0).reshape(db, steps, PAGES_PER_STEP)
    step_id = jnp.arange(steps, dtype=I32)[None, :, None]
    last = lax.cummax(jnp.where(needed, step_id, 0), axis=1)
    return jnp.take_along_axis(page_table.reshape(db, steps, PAGES_PER_STEP), last, axis=1).reshape(-1)


def _sample_attend(vncol, probs, stat, need, cache_vt, page_table):
    db, n_pages = page_table.shape
    page = cache_vt.shape[-1]
    ppb = MOBA_BLOCK // page
    steps = n_pages // PAGES_PER_STEP
    nb = n_pages // ppb
    span = PAGES_PER_STEP * page
    need2 = need.reshape(db, LANES)[:, :nb]
    col = lambda: pl.BlockSpec((1, N_HEADS, HEAD_DIM, 1), lambda b, s, *_: (b, 0, 0, 0))
    grid_spec = pltpu.PrefetchScalarGridSpec(
        num_scalar_prefetch=2,
        grid=(db, steps),
        in_specs=[col(),
                  pl.BlockSpec((1, N_HEADS, LANES), lambda b, s, *_: (b, 0, 0)),
                  pl.BlockSpec((1, 1, N_HEADS, span), lambda b, s, *_: (b, s, 0, 0))] + _page_specs(n_pages),
        out_specs=col(),
        scratch_shapes=[pltpu.VMEM((N_HEADS, HEAD_DIM, page), F32)],
    )
    out = pl.pallas_call(
        functools.partial(_sample_attend_kernel, page=page, pages_per_block=ppb, nb=nb, steps=steps),
        grid_spec=grid_spec,
        out_shape=jax.ShapeDtypeStruct((db, N_HEADS, HEAD_DIM, 1), F32),
        compiler_params=_cparams(("parallel", "arbitrary")),
        name="sample_attend",
    )(_fetched_pages(page_table, need2, ppb), need2.reshape(-1), vncol, stat, probs, *([cache_vt] * PAGES_PER_STEP))
    return out.reshape(db, N_HEADS * HEAD_DIM)


CONV_HALO = 32


def _ln_silu(y, g, b):
    mu = jnp.mean(y, axis=-1, keepdims=True)
    var = jnp.mean(jnp.square(y - mu), axis=-1, keepdims=True)
    n = (y - mu) * lax.rsqrt(var + EPS) * g + b
    return n * _sigmoid(n)


def _conv_prompt_kernel(zc_ref, zp_ref, dw_ref, db_ref, g_ref, b_ref, o_ref, win_ref, *, cw, tile):
    i = pl.program_id(1)
    prev = zp_ref[0, tile - CONV_HALO:tile, :]
    win_ref[0:CONV_HALO, :] = jnp.where(i > 0, prev, 0.0)
    win_ref[CONV_HALO:CONV_HALO + tile, :] = zc_ref[0]
    off = CONV_HALO - (cw - 1)
    y = jnp.broadcast_to(db_ref[...], (tile, db_ref.shape[-1]))
    for w in range(cw):
        y = y + win_ref[off + w:off + w + tile, :] * dw_ref[w:w + 1, :]
    o_ref[0] = _ln_silu(y, g_ref[...], b_ref[...])


def _conv_prompt(z, dw_w, dw_b, ln_g, ln_b, tile=256):
    b, s, c = z.shape
    cw = dw_w.shape[0]
    assert s % tile == 0 and cw - 1 <= CONV_HALO <= tile
    full = lambda a: pl.BlockSpec(a.shape, lambda bi, i: (0,) * a.ndim)
    return pl.pallas_call(
        functools.partial(_conv_prompt_kernel, cw=cw, tile=tile),
        grid=(b, s // tile),
        in_specs=[pl.BlockSpec((1, tile, c), lambda bi, i: (bi, i, 0)),
                  pl.BlockSpec((1, tile, c), lambda bi, i: (bi, jnp.maximum(i - 1, 0), 0)),
                  full(dw_w), full(dw_b), full(ln_g), full(ln_b)],
        out_specs=pl.BlockSpec((1, tile, c), lambda bi, i: (bi, i, 0)),
        out_shape=jax.ShapeDtypeStruct((b, s, c), F32),
        scratch_shapes=[pltpu.VMEM((CONV_HALO + tile, c), F32)],
        compiler_params=_cparams(("parallel", "arbitrary")),
        name="conv_prompt",
    )(z, z, dw_w, dw_b, ln_g, ln_b)


SAMPLE_CONV_ROWS = 8


def _conv_sample_kernel(st_ref, z_ref, dw_ref, db_ref, g_ref, b_ref, o_ref, ns_ref, *, cw):
    rows = []
    for r in range(SAMPLE_CONV_ROWS):
        y = jnp.sum(st_ref[r] * dw_ref[0:cw - 1, :], axis=0, keepdims=True)
        rows.append(y + z_ref[r:r + 1, :] * dw_ref[cw - 1:cw, :] + db_ref[...])
        ns_ref[r, 0:cw - 2, :] = st_ref[r, 1:cw - 1, :]
        ns_ref[r, cw - 2:cw - 1, :] = z_ref[r:r + 1, :]
    o_ref[...] = _ln_silu(jnp.concatenate(rows, axis=0), g_ref[...], b_ref[...])


def _conv_sample(state, z, dw_w, dw_b, ln_g, ln_b):
    db, sw, c = state.shape
    cw = dw_w.shape[0]
    r = SAMPLE_CONV_ROWS
    assert sw == cw - 1 and db % r == 0
    full = lambda a: pl.BlockSpec(a.shape, lambda i: (0,) * a.ndim)
    return pl.pallas_call(
        functools.partial(_conv_sample_kernel, cw=cw),
        grid=(db // r,),
        in_specs=[pl.BlockSpec((r, sw, c), lambda i: (i, 0, 0)), pl.BlockSpec((r, c), lambda i: (i, 0)),
                  full(dw_w), full(dw_b), full(ln_g), full(ln_b)],
        out_specs=[pl.BlockSpec((r, c), lambda i: (i, 0)), pl.BlockSpec((r, sw, c), lambda i: (i, 0, 0))],
        out_shape=[jax.ShapeDtypeStruct((db, c), F32), jax.ShapeDtypeStruct((db, sw, c), F32)],
        compiler_params=_cparams(("parallel",)),
        name="conv_sample",
    )(state, z, dw_w, dw_b, ln_g, ln_b)


def _outproj_kernel(x_ref, att_ref, conv_ref, sga_ref, sgb_ref, wao_ref, wco_ref, wo_ref, g2_ref, wq_ref,
                    x1_ref, hn_ref, pq_ref):
    merged = sga_ref[...] * _dot(att_ref[...], wao_ref[...]) + sgb_ref[...] * _dot(conv_ref[...], wco_ref[...])
    x1 = x_ref[...] + _dot(merged, wo_ref[...])
    x1_ref[...] = x1
    hn = _rms(x1, g2_ref[...])
    hn_ref[...] = hn
    pq_ref[...] = _dot(hn, wq_ref[...])


def _outproj(x2, att, conv, sga, sgb, wao, wco, wo, g2, wq, tm):
    t, d = x2.shape
    assert t % tm == 0
    row = lambda a: pl.BlockSpec((tm, a.shape[1]), lambda i: (i, 0))
    full = lambda a: pl.BlockSpec(a.shape, lambda i: (0,) * a.ndim)
    acts = (x2, att, conv, sga, sgb)
    wts = (wao, wco, wo, g2, wq)
    pq_w = wq.shape[1]
    return pl.pallas_call(
        _outproj_kernel,
        grid=(t // tm,),
        in_specs=[row(a) for a in acts] + [full(a) for a in wts],
        out_specs=[pl.BlockSpec((tm, d), lambda i: (i, 0)), pl.BlockSpec((tm, d), lambda i: (i, 0)),
                   pl.BlockSpec((tm, pq_w), lambda i: (i, 0))],
        out_shape=[jax.ShapeDtypeStruct((t, d), F32), jax.ShapeDtypeStruct((t, d), F32),
                   jax.ShapeDtypeStruct((t, pq_w), F32)],
        compiler_params=_cparams(("parallel",)),
        name="outproj",
    )(*acts, *wts)


PEER_TILE = 128
CAND_ROWS = PEER_TOPK + 8 * SUBLANES


def _peer_route_kernel(pq_ref, k12_ref, eidx_ref, pidx_ref, shift_ref, gate_ref, s_ref, c_ref, ci_ref, *, n_keys):
    kk = PEER_TOPK
    t = PEER_TILE
    nh = PEER_HEADS
    half = kk // 2
    assert kk == 16 and half == SUBLANES
    pair_span = n_keys * n_keys // 2
    key_id = lax.broadcasted_iota(I32, (n_keys, t), 0)
    kidx = lax.broadcasted_iota(I32, (kk, t), 0)
    for h in range(nh):
        qh = pq_ref[:, h * LANES:(h + 1) * LANES]
        s12 = _dot_nt(k12_ref[...], qh, HIGHEST)
        s_ref[2 * h] = s12[0:n_keys]
        s_ref[2 * h + 1] = s12[n_keys:2 * n_keys]

    def level1(it, carry):
        vals, idxs = carry
        nv, ni = [], []
        for c in range(2 * nh):
            s = s_ref[c]
            mx = jnp.max(s, axis=0, keepdims=True)
            pick = jnp.min(jnp.where(s == mx, key_id, n_keys), axis=0, keepdims=True)
            s_ref[c] = jnp.where(key_id == pick, -jnp.inf, s)
            nv.append(jnp.where(kidx == it, mx, vals[c]))
            ni.append(jnp.where(kidx == it, pick, idxs[c]))
        return tuple(nv), tuple(ni)

    zf = tuple(jnp.zeros((kk, t), F32) for _ in range(2 * nh))
    zi = tuple(jnp.zeros((kk, t), I32) for _ in range(2 * nh))
    vals, idxs = lax.fori_loop(0, kk, level1, (zf, zi))

    b16 = lax.broadcasted_iota(I32, (kk, t), 0)
    b8 = lax.broadcasted_iota(I32, (half, t), 0)
    flat_parts = [b16, kk + b8]
    for a in range(2, half):
        flat_parts.append(a * kk + b8)
    flat_parts.append((half + b8) * kk)
    flat = jnp.concatenate(flat_parts, axis=0)
    for h in range(nh):
        v1, v2, i1, i2 = vals[2 * h], vals[2 * h + 1], idxs[2 * h], idxs[2 * h + 1]
        c_parts = [v1[0:1] + v2, v1[1:2] + v2[0:half]]
        i_parts = [i1[0:1] * n_keys + i2, i1[1:2] * n_keys + i2[0:half]]
        for a in range(2, half):
            c_parts.append(jnp.where(b8 < kk // (a + 1), v1[a:a + 1] + v2[0:half], -jnp.inf))
            i_parts.append(i1[a:a + 1] * n_keys + i2[0:half])
        c_parts.append(v1[half:kk] + v2[0:1])
        i_parts.append(i1[half:kk] * n_keys + i2[0:1])
        c_ref[h] = jnp.concatenate(c_parts, axis=0)
        ci_ref[h] = jnp.concatenate(i_parts, axis=0)

    def level2(it, carry):
        vals2, exps = carry
        nv, ne = [], []
        for h in range(nh):
            c = c_ref[h]
            mx = jnp.max(c, axis=0, keepdims=True)
            pick = jnp.min(jnp.where(c == mx, flat, kk * kk), axis=0, keepdims=True)
            hit = flat == pick
            e = jnp.max(jnp.where(hit, ci_ref[h], -1), axis=0, keepdims=True)
            c_ref[h] = jnp.where(hit, -jnp.inf, c)
            nv.append(jnp.where(kidx == it, mx, vals2[h]))
            ne.append(jnp.where(kidx == it, e, exps[h]))
        return tuple(nv), tuple(ne)

    sc, eidx = lax.fori_loop(0, kk, level2, (zf[:nh], zi[:nh]))
    for h in range(nh):
        ex = jnp.exp(sc[h] - jnp.max(sc[h], axis=0, keepdims=True))
        rows = slice(h * kk, (h + 1) * kk)
        gate_ref[0, rows, :] = ex / jnp.sum(ex, axis=0, keepdims=True)
        eidx_ref[0, rows, :] = eidx[h]
        upper = (eidx[h] >= pair_span).astype(I32)
        pidx_ref[0, rows, :] = eidx[h] - upper * pair_span
        shift_ref[0, rows, :] = (1 - upper) * 16


def _peer_route(pq, k1ext, k2ext):
    t, w = pq.shape
    n_keys = k1ext.shape[0]
    slots = PEER_HEADS * PEER_TOPK
    assert t % PEER_TILE == 0 and w == PEER_HEADS * LANES and k1ext.shape[1] == LANES
    k12 = jnp.concatenate([k1ext, k2ext], axis=0)
    nt = t // PEER_TILE
    full = lambda a: pl.BlockSpec(a.shape, lambda i: (0,) * a.ndim)
    ospec = pl.BlockSpec((1, slots, PEER_TILE), lambda i: (i, 0, 0))
    ishape = jax.ShapeDtypeStruct((nt, slots, PEER_TILE), I32)
    return pl.pallas_call(
        functools.partial(_peer_route_kernel, n_keys=n_keys),
        grid=(nt,),
        in_specs=[pl.BlockSpec((PEER_TILE, w), lambda i: (i, 0)), full(k12)],
        out_specs=[ospec, ospec, ospec, ospec],
        out_shape=[ishape, ishape, ishape, jax.ShapeDtypeStruct((nt, slots, PEER_TILE), F32)],
        scratch_shapes=[pltpu.VMEM((2 * PEER_HEADS, n_keys, PEER_TILE), F32),
                        pltpu.VMEM((PEER_HEADS, CAND_ROWS, PEER_TILE), F32),
                        pltpu.VMEM((PEER_HEADS, CAND_ROWS, PEER_TILE), I32)],
        compiler_params=_cparams(("parallel",)),
        name="peer_route",
    )(pq, k12)


HI_MASK = np.uint32(0xFFFF0000)
HALF_SUB = SUBLANES // 2


def _as_f32(bits):
    return lax.bitcast_convert_type(bits, F32)


def _tile_copies(hbm_refs, smem_refs, sems, tile, slot, n):
    dst = pl.ds(pl.multiple_of(slot * n, n), n)
    return [pltpu.make_async_copy(h.at[tile], s.at[dst], sems.at[k, slot])
            for k, (h, s) in enumerate(zip(hbm_refs, smem_refs))]


def _stream_tiles(hbm_refs, smem_refs, sems, n_tiles, n):
    i = pl.program_id(0)
    slot = i % 2

    @pl.when(i == 0)
    def _():
        for c in _tile_copies(hbm_refs, smem_refs, sems, 0, 0, n):
            c.start()

    @pl.when(i + 1 < n_tiles)
    def _():
        for c in _tile_copies(hbm_refs, smem_refs, sems, i + 1, 1 - slot, n):
            c.start()

    for c in _tile_copies(hbm_refs, smem_refs, sems, i, slot, n):
        c.wait()
    return slot


SCORE_CHUNK_GROUPS = 8


def _peer_score_kernel(pidx_hbm, shift_hbm, x_ref, gate_ref, tab_ref, act_ref, pidx_sm, shift_sm, dots_ref, grp_ref, sems,
                       *, n_tiles, slots):
    tt = PEER_TILE
    n = slots * tt
    slot = _stream_tiles((pidx_hbm, shift_hbm), (pidx_sm, shift_sm), sems, n_tiles, n)
    lane = lax.broadcasted_iota(I32, (slots, tt), 1)
    dots_ref[...] = jnp.zeros_like(dots_ref)
    chunk_rows = SCORE_CHUNK_GROUPS * SUBLANES
    span = chunk_rows * tt
    view_len = 2 * n - (chunk_rows - 1) * tt

    lane_c = lax.broadcasted_iota(I32, (chunk_rows, tt), 1)
    grp_ref[...] = jnp.zeros_like(grp_ref)

    def finish(tok, buf, c):
        rows = pl.ds(pl.multiple_of(c * chunk_rows, chunk_rows), chunk_rows)
        part = grp_ref[buf, pl.ds(c * SCORE_CHUNK_GROUPS, SCORE_CHUNK_GROUPS)]
        col = jnp.sum(part.reshape(chunk_rows, LANES), axis=-1, keepdims=True)
        dots_ref[rows, :] = jnp.where(lane_c == tok, col, dots_ref[rows, :])

    def token(t, carry):
        base = slot * n + t
        x = x_ref[t]
        cur = t % 2

        def chunk(c, inner):
            finish(t - 1, 1 - cur, c)
            cbase = base + c * span
            for g in range(SCORE_CHUNK_GROUPS):
                prods = []
                for r in range(SUBLANES):
                    view = pl.ds((g * SUBLANES + r) * tt, view_len)
                    w = tab_ref[pidx_sm.at[view][cbase]]
                    prods.append(_as_f32((w << shift_sm.at[view][cbase].astype(U32)) & HI_MASK) * x)
                grp_ref[cur, c * SCORE_CHUNK_GROUPS + g] = _sublane_sums(prods)
            return inner

        lax.fori_loop(0, slots // chunk_rows, chunk, 0)
        return carry

    lax.fori_loop(0, tt, token, 0)
    for c in range(slots // chunk_rows):
        finish(tt - 1, (tt - 1) % 2, c)
    act_ref[0] = _gelu_erf(dots_ref[...]) * gate_ref[0]


def _peer_scores(x3, pidx, shift, gate, tab):
    nt, slots, tt = gate.shape
    n = slots * tt
    flat = lambda a: a.reshape(nt, n)
    kern = functools.partial(_peer_score_kernel, n_tiles=nt, slots=slots)
    return pl.pallas_call(
        kern,
        grid=(nt,),
        in_specs=[pl.BlockSpec(memory_space=pl.ANY), pl.BlockSpec(memory_space=pl.ANY),
                  pl.BlockSpec((tt, SUBLANES, LANES), lambda i: (i, 0, 0)),
                  pl.BlockSpec((1, slots, tt), lambda i: (i, 0, 0)),
                  pl.BlockSpec(memory_space=pltpu.VMEM)],
        out_specs=pl.BlockSpec((1, slots, tt), lambda i: (i, 0, 0)),
        out_shape=jax.ShapeDtypeStruct((nt, slots, tt), F32),
        scratch_shapes=[pltpu.SMEM((2 * n,), I32), pltpu.SMEM((2 * n,), I32),
                        pltpu.VMEM((slots, tt), F32), pltpu.VMEM((2, slots // SUBLANES, SUBLANES, LANES), F32),
                        pltpu.SemaphoreType.DMA((2, 2))],
        compiler_params=_cparams(("arbitrary",)),
        name="peer_scores",
    )(flat(pidx), flat(shift), x3, gate, tab)


MIX_ACCS = 4


MIX_CHUNK = 64


def _peer_mix_kernel(eidx_hbm, act_hbm, tab_ref, o_ref, eidx_sm, act_sm, sems, *, n_tiles, slots):
    tt = PEER_TILE
    n = slots * tt
    slot = _stream_tiles((eidx_hbm, act_hbm), (eidx_sm, act_sm), sems, n_tiles, n)
    span = MIX_CHUNK * tt
    view_len = 2 * n - (MIX_CHUNK - 1) * tt

    def token(t, carry):
        base = slot * n + t

        def chunk(c, accs):
            lo, hi = list(accs[0]), list(accs[1])
            cbase = base + c * span
            for r in range(MIX_CHUNK):
                view = pl.ds(r * tt, view_len)
                w = tab_ref[eidx_sm.at[view][cbase]]
                a = act_sm.at[view][cbase]
                lo[r % MIX_ACCS] = lo[r % MIX_ACCS] + a * _as_f32(w << 16)
                hi[r % MIX_ACCS] = hi[r % MIX_ACCS] + a * _as_f32(w & HI_MASK)
            return tuple(lo), tuple(hi)

        zero = tuple(jnp.zeros((HALF_SUB, LANES), F32) for _ in range(MIX_ACCS))
        lo, hi = lax.fori_loop(0, slots // MIX_CHUNK, chunk, (zero, zero))
        o_ref[t, 0:HALF_SUB, :] = (lo[0] + lo[1]) + (lo[2] + lo[3])
        o_ref[t, HALF_SUB:SUBLANES, :] = (hi[0] + hi[1]) + (hi[2] + hi[3])
        return carry

    lax.fori_loop(0, tt, token, 0)


def _peer_mix(eidx, act, tab):
    nt, slots, tt = act.shape
    n = slots * tt
    flat = lambda a: a.reshape(nt, n)
    kern = functools.partial(_peer_mix_kernel, n_tiles=nt, slots=slots)
    return pl.pallas_call(
        kern,
        grid=(nt,),
        in_specs=[pl.BlockSpec(memory_space=pl.ANY), pl.BlockSpec(memory_space=pl.ANY),
                  pl.BlockSpec(memory_space=pltpu.VMEM)],
        out_specs=pl.BlockSpec((tt, SUBLANES, LANES), lambda i: (i, 0, 0)),
        out_shape=jax.ShapeDtypeStruct((nt * tt, SUBLANES, LANES), F32),
        scratch_shapes=[pltpu.SMEM((2 * n,), I32), pltpu.SMEM((2 * n,), F32), pltpu.SemaphoreType.DMA((2, 2))],
        compiler_params=_cparams(("arbitrary",)),
        name="peer_mix",
    )(flat(eidx), flat(act), tab)


def _bf16_bits(tab):
    return lax.bitcast_convert_type(tab.astype(BF16), jnp.uint16).astype(U32)


def _pack_pairs(tab):
    n, d = tab.shape
    assert n % 2 == 0 and d == SUBLANES * LANES
    bits = _bf16_bits(tab)
    return (bits[:n // 2] | (bits[n // 2:] << 16)).reshape(n // 2, SUBLANES, LANES)


def _pack_halves(tab):
    n, d = tab.shape
    assert d == SUBLANES * LANES
    bits = _bf16_bits(tab)
    return (bits[:, :d // 2] | (bits[:, d // 2:] << 16)).reshape(n, HALF_SUB, LANES)


def _peer(hn, pq, k1ext, k2ext, u_pack, v_pack):
    t, d = hn.shape
    eidx, pidx, shift, gate = _peer_route(pq, k1ext, k2ext)
    act = _peer_scores(hn.reshape(t, SUBLANES, LANES), pidx, shift, gate, u_pack)
    return _peer_mix(eidx, act, v_pack).reshape(t, d)


def _ple_kernel(x1_ref, po_ref, pe_ref, g3_ref, wg_ref, wp_ref, gf_ref, y_ref, *, final):
    x2 = x1_ref[...] + po_ref[...]
    gate = _sigmoid(_dot(_rms(x2, g3_ref[...]), wg_ref[...]))
    x3 = x2 + gate * _dot(pe_ref[...], wp_ref[...])
    y_ref[...] = _rms(x3, gf_ref[...]) if final else x3


def _ple(x1, po, pe, g3, wg, wp, gf, tm, final):
    t, d = x1.shape
    assert t % tm == 0
    row = lambda a: pl.BlockSpec((tm, a.shape[1]), lambda i: (i, 0))
    full = lambda a: pl.BlockSpec(a.shape, lambda i: (0,) * a.ndim)
    return pl.pallas_call(
        functools.partial(_ple_kernel, final=final),
        grid=(t // tm,),
        in_specs=[row(x1), row(po), row(pe), full(g3), full(wg), full(wp), full(gf)],
        out_specs=pl.BlockSpec((tm, d), lambda i: (i, 0)),
        out_shape=jax.ShapeDtypeStruct((t, d), F32),
        compiler_params=_cparams(("parallel",)),
        name="ple_final",
    )(x1, po, pe, g3, wg, wp, gf)


def kernel(x_prompt, x_sample, cache_k, cache_v, state_conv, page_table, p_prompt, p_sample, norm1_g, w_in, dw_w, dw_b, cln_g, cln_b, w_att_out, w_conv_out, w_o, norm2_g, peer_wq, peer_k1, peer_k2, peer_u, peer_v, norm3_g, w_ple, w_ple_gate, final_g):
    depth = w_in.shape[0]
    b, s, d = x_prompt.shape
    db, ds, _ = x_sample.shape
    page = cache_k.shape[2]
    n_pages = page_table.shape[1]
    att_w = N_HEADS * HEAD_DIM
    assert ds == 1 and (n_pages * page) % MOBA_BLOCK == 0
    row = lambda a: a.reshape(1, -1)
    pt_flat = page_table.reshape(-1).astype(I32)
    gf = row(final_g)
    assert 2 * peer_k1.shape[-1] == LANES

    xp = x_prompt.reshape(b * s, d)
    xs = x_sample.reshape(db, d)
    outs = [[] for _ in range(6)]
    for li in range(depth):
        last = li == depth - 1
        bf = lambda a: a[li].astype(BF16)
        w_in_b, wao, wco, wo, wq, wpg, wpl = (bf(a) for a in (w_in, w_att_out, w_conv_out, w_o, peer_wq, w_ple_gate, w_ple))
        zeros = jnp.zeros_like(peer_k1[li])
        k1ext = jnp.concatenate([peer_k1[li], zeros], axis=1)
        k2ext = jnp.concatenate([zeros, peer_k2[li]], axis=1)
        u_pack = _pack_pairs(peer_u[li])
        v_pack = _pack_halves(peer_v[li])
        conv_w = (dw_w[li], row(dw_b[li]), row(cln_g[li]), row(cln_b[li]))
        g1, g2, g3 = row(norm1_g[li]), row(norm2_g[li]), row(norm3_g[li])
        ckt = jnp.transpose(cache_k[li], (0, 2, 3, 1))
        cvt = jnp.transpose(cache_v[li], (0, 2, 3, 1))

        qs, ks, vs, zs, sgas, sgbs = _inproj(xs, g1, w_in_b, db)
        col = lambda a: a.reshape(db, N_HEADS, HEAD_DIM, 1)
        probs, stat, need = _sample_scores(qs, ks, col(qs), ckt, pt_flat, n_pages)
        atts = _sample_attend(col(vs), probs, stat, need, cvt, page_table.astype(I32))
        convs, new_state = _conv_sample(state_conv[li], zs, *conv_w)
        x1s, hns, pqs = _outproj(xs, atts, convs, sgas, sgbs, wao, wco, wo, g2, wq, db)
        pos = _peer(hns, pqs, k1ext, k2ext, u_pack, v_pack)
        xs = _ple(x1s, pos, p_sample[li].reshape(db, -1), g3, wpg, wpl, gf, db, last)
        outs[3].append(ks.reshape(db, ds, N_HEADS, HEAD_DIM))
        outs[4].append(vs.reshape(db, ds, N_HEADS, HEAD_DIM))
        outs[5].append(new_state)

        q, k, v, z, sga, sgb, kt, vt = _inproj(xp, g1, w_in_b, 256, seq_len=s)
        att = _moba_prompt(q.reshape(b, s, att_w), k.reshape(b, s, att_w), v.reshape(b, s, att_w))
        z3 = z.reshape(b, s, -1)
        conv = _conv_prompt(z3, *conv_w)
        x1, hn, pq = _outproj(xp, att.reshape(b * s, att_w), conv.reshape(b * s, -1), sga, sgb, wao, wco, wo, g2, wq, 256)
        po = _peer(hn, pq, k1ext, k2ext, u_pack, v_pack)
        xp = _ple(x1, po, p_prompt[li].reshape(b * s, -1), g3, wpg, wpl, gf, 256, last)
        outs[0].append(jnp.transpose(kt, (0, 3, 1, 2)))
        outs[1].append(jnp.transpose(vt, (0, 3, 1, 2)))
        outs[2].append(z3[:, s - (dw_w.shape[1] - 1):, :])

    return (xp.reshape(b, s, d), xs.reshape(db, ds, d)) + tuple(jnp.stack(o) for o in outs)
```

```python
import functools

import numpy as np
import jax
import jax.numpy as jnp
from jax import lax
from jax.experimental import pallas as pl
from jax.experimental.pallas import tpu as pltpu

F32 = jnp.float32
BF16 = jnp.bfloat16
I32 = jnp.int32
U32 = jnp.uint32

N_HEADS = 8
HEAD_DIM = 64
MOBA_BLOCK = 256
MOBA_TOPK = 3
PEER_HEADS = 8
PEER_TOPK = 16
EPS = 1e-6
NEG = -1e30
ATT_SCALE = HEAD_DIM ** -0.5

LANES = 128
SUBLANES = 8
VMEM_LIMIT_BYTES = 56 * 1024 * 1024

HIGHEST = lax.Precision.HIGHEST


def _cparams(sem):
    return pltpu.CompilerParams(dimension_semantics=sem, vmem_limit_bytes=VMEM_LIMIT_BYTES)


def _rms(x, g):
    return x * lax.rsqrt(jnp.mean(x * x, axis=-1, keepdims=True) + EPS) * g


def _sigmoid(x):
    return 1.0 / (1.0 + jnp.exp(-x))


def _dot(a, b):
    return jnp.dot(a.astype(BF16), b.astype(BF16), preferred_element_type=F32)


def _dot_nt(a, b, precision=None):
    return lax.dot_general(a, b, (((1,), (1,)), ((), ())), preferred_element_type=F32, precision=precision)


def _gelu_erf(x):
    return 0.5 * x * (1.0 + lax.erf(x * np.float32(1.0 / np.sqrt(2.0))))


def _inproj_kernel(x_ref, g_ref, w_ref, q_ref, k_ref, v_ref, z_ref, sga_ref, sgb_ref, *t_refs, att_w, conv_ch, d_model):
    h = _rms(x_ref[...], g_ref[...]).astype(BF16)

    def proj(lo, width):
        return jnp.dot(h, w_ref[:, lo:lo + width], preferred_element_type=F32)

    q_ref[...] = proj(0, att_w)
    k = proj(att_w, att_w)
    v = proj(2 * att_w, att_w)
    k_ref[...] = k
    v_ref[...] = v
    if t_refs:
        kt_ref, vt_ref = t_refs
        kt_ref[0] = k.T.reshape(N_HEADS, HEAD_DIM, k.shape[0])
        vt_ref[0] = v.T.reshape(N_HEADS, HEAD_DIM, v.shape[0])
    base = 3 * att_w
    z_ref[...] = proj(base, conv_ch) * _sigmoid(proj(base + conv_ch, conv_ch))
    base += 2 * conv_ch
    sga_ref[...] = _sigmoid(proj(base, d_model))
    sgb_ref[...] = _sigmoid(proj(base + d_model, d_model))


def _inproj(x2, g, w_bf, tm, seq_len=None):
    t, d = x2.shape
    att_w = N_HEADS * HEAD_DIM
    conv_ch = d // 2
    assert w_bf.shape[1] == 3 * att_w + 2 * conv_ch + 2 * d and t % tm == 0
    row = lambda w: pl.BlockSpec((tm, w), lambda i: (i, 0))
    full = lambda a: pl.BlockSpec(a.shape, lambda i: (0,) * a.ndim)
    out_w = (att_w, att_w, att_w, conv_ch, d, d)
    out_specs = [row(w) for w in out_w]
    out_shape = [jax.ShapeDtypeStruct((t, w), F32) for w in out_w]
    if seq_len is not None:
        assert seq_len % tm == 0 and t % seq_len == 0 and tm % LANES == 0
        tiles = seq_len // tm
        t_spec = pl.BlockSpec((1, N_HEADS, HEAD_DIM, tm), lambda i: (i // tiles, 0, 0, i % tiles))
        t_shape = jax.ShapeDtypeStruct((t // seq_len, N_HEADS, HEAD_DIM, seq_len), F32)
        out_specs += [t_spec, t_spec]
        out_shape += [t_shape, t_shape]
    return pl.pallas_call(
        functools.partial(_inproj_kernel, att_w=att_w, conv_ch=conv_ch, d_model=d),
        grid=(t // tm,),
        in_specs=[row(d), full(g), full(w_bf)],
        out_specs=out_specs,
        out_shape=out_shape,
        compiler_params=_cparams(("parallel",)),
        name="inproj",
    )(x2, g, w_bf)


def _moba_prompt_kernel(q_ref, k_ref, v_ref, o_ref, kmean_ref, sel_ref, *, nb, rq):
    blk = MOBA_BLOCK
    i = pl.program_id(2)
    own = (i * rq) // blk
    off = (i * rq) % blk

    @pl.when(i == 0)
    def _():
        for j in range(nb):
            kmean_ref[j:j + 1, :] = jnp.sum(k_ref[0, j * blk:(j + 1) * blk, :], axis=0, keepdims=True) * (1.0 / blk)

    lane = lax.broadcasted_iota(I32, (1, LANES), 1)
    row_i = lax.broadcasted_iota(I32, (rq, blk), 0) + off
    col_i = lax.broadcasted_iota(I32, (rq, blk), 1)
    causal = col_i <= row_i
    kmean = kmean_ref[...]
    bidx = lax.broadcasted_iota(I32, (nb, rq), 0)
    expand = (lax.broadcasted_iota(I32, (LANES, 2 * nb * LANES), 1) // LANES
              == lax.broadcasted_iota(I32, (LANES, 2 * nb * LANES), 0)).astype(BF16)
    head_a = lane < HEAD_DIM
    q_all = q_ref[0]
    kmean2 = jnp.concatenate([jnp.where(head_a, kmean, 0.0), jnp.where(head_a, 0.0, kmean)], axis=0)
    gates2 = _dot_nt(kmean2, q_all, HIGHEST)
    sel_parts = []
    for hs in range(2):
        gates = jnp.where(bidx < own, gates2[hs * nb:(hs + 1) * nb], NEG)
        rank = jnp.zeros((nb, rq), F32)
        for m in range(nb):
            gm = gates[m:m + 1, :]
            ahead = ((bidx > m) & (gm >= gates)) | ((bidx < m) & (gm > gates))
            rank = rank + ahead.astype(F32)
        sel_parts.append(jnp.where((rank < MOBA_TOPK) & (bidx < own), 1.0, 0.0))
    sel_q = jnp.concatenate(sel_parts + [jnp.zeros((LANES - 2 * nb, rq), F32)], axis=0).T
    sel_all = _dot(sel_q, expand)
    for hs in range(2):
        for j in range(nb):
            slab = hs * nb + j
            sel_ref[hs, j] = sel_all[:, slab * LANES:(slab + 1) * LANES]
    qbs = [jnp.where(head_a, q_all, 0.0).astype(BF16), jnp.where(head_a, 0.0, q_all).astype(BF16)]

    def step(s, mask, vb, carry):
        m_run, l_run, acc = carry
        s = jnp.where(mask, s * ATT_SCALE, NEG)
        m_new = jnp.maximum(m_run, jnp.max(s, axis=-1, keepdims=True))
        p = jnp.where(mask, jnp.exp(s - m_new), 0.0)
        alpha = jnp.exp(m_run - m_new)
        l_new = alpha * l_run + jnp.sum(p, axis=-1, keepdims=True)
        acc_new = alpha * acc + jnp.dot(p.astype(BF16), vb, preferred_element_type=F32)
        return m_new, l_new, acc_new

    def both_heads(j, n_blocks, masks, carries):
        start = pl.multiple_of(j * blk, blk)
        kb = k_ref[0, pl.ds(start, n_blocks * blk), :].astype(BF16)
        vb = v_ref[0, pl.ds(start, n_blocks * blk), :].astype(BF16)
        return tuple(step(_dot_nt(qbs[hs], kb), masks[hs], vb, carries[hs]) for hs in range(2))

    def past_mask(hs, j):
        sj = sel_ref[hs, j]
        return jnp.concatenate([sj, sj], axis=1) > 0.5

    def pair_body(jj, carries):
        masks = [jnp.concatenate([past_mask(hs, 2 * jj), past_mask(hs, 2 * jj + 1)], axis=1) for hs in range(2)]
        return both_heads(2 * jj, 2, masks, carries)

    def single_body(j, carries):
        return both_heads(j, 1, [past_mask(hs, j) for hs in range(2)], carries)

    init = (jnp.full((rq, 1), NEG, F32), jnp.zeros((rq, 1), F32), jnp.zeros((rq, LANES), F32))
    pairs = own // 2
    carries = lax.fori_loop(0, pairs, pair_body, (init, init))
    carries = lax.fori_loop(2 * pairs, own, single_body, carries)
    (_, l_a, acc_a), (_, l_b, acc_b) = both_heads(own, 1, (causal, causal), carries)
    o_ref[0] = jnp.where(lane < HEAD_DIM, acc_a / l_a, acc_b / l_b)


MOBA_Q_ROWS = 256


def _moba_prompt(q, k, v):
    b, s, w = q.shape
    blk = MOBA_BLOCK
    rq = MOBA_Q_ROWS
    nb = s // blk
    assert s % blk == 0 and blk % rq == 0 and rq % LANES == 0 and MOBA_TOPK <= nb and 2 * nb <= LANES
    assert w % LANES == 0 and LANES == 2 * HEAD_DIM and blk == 2 * LANES
    kv_spec = pl.BlockSpec((1, s, LANES), lambda bi, hp, i: (bi, 0, hp))
    q_spec = pl.BlockSpec((1, rq, LANES), lambda bi, hp, i: (bi, i, hp))
    return pl.pallas_call(
        functools.partial(_moba_prompt_kernel, nb=nb, rq=rq),
        grid=(b, w // LANES, s // rq),
        in_specs=[q_spec, kv_spec, kv_spec],
        out_specs=q_spec,
        out_shape=jax.ShapeDtypeStruct((b, s, w), F32),
        scratch_shapes=[pltpu.VMEM((nb, LANES), F32), pltpu.VMEM((2, nb, rq, LANES), F32)],
        compiler_params=_cparams(("parallel", "parallel", "arbitrary")),
        name="moba_prompt",
    )(q, k, v)


PAGES_PER_STEP = 16


def _sublane_sums(vs):
    sub = lax.broadcasted_iota(I32, (SUBLANES, LANES), 0)
    cur = [vs[_BUTTERFLY_ORDER[p]] for p in range(SUBLANES)]
    for shift, keep in ((1, sub % 2 == 0), (2, sub % 4 < 2), (4, sub < 4)):
        nxt = []
        for a, b in zip(cur[0::2], cur[1::2]):
            nxt.append(jnp.where(keep, a, b) + pltpu.roll(jnp.where(keep, b, a), shift, 0))
        cur = nxt
    return cur[0]


def _butterfly_order():
    ids = [np.full((SUBLANES,), i) for i in range(SUBLANES)]
    sub = np.arange(SUBLANES)
    for shift, keep in ((1, sub % 2 == 0), (2, sub % 4 < 2), (4, sub < 4)):
        nxt = []
        for a, b in zip(ids[0::2], ids[1::2]):
            first = np.where(keep, a, b)
            assert (first == np.roll(np.where(keep, b, a), shift)).all()
            nxt.append(first)
        ids = nxt
    order = np.empty(SUBLANES, np.int64)
    order[ids[0]] = np.arange(SUBLANES)
    return [int(v) for v in order]


_BUTTERFLY_ORDER = _butterfly_order()


def _head_rows(row):
    w = row.shape[-1]
    lane_head = lax.broadcasted_iota(I32, (N_HEADS, w), 1) // HEAD_DIM
    sub = lax.broadcasted_iota(I32, (N_HEADS, w), 0)
    return jnp.where(lane_head == sub, jnp.broadcast_to(row, (N_HEADS, w)), 0.0)


def _sample_scores_kernel(pt_ref, qcol_ref, q_ref, kn_ref, *refs, page, pages_per_block, nb, steps):
    del pt_ref
    k_refs = refs[:PAGES_PER_STEP]
    p_ref, stat_ref, need_ref, qb_ref, gacc_ref = refs[PAGES_PER_STEP:]
    st = pl.program_id(1)
    bps = PAGES_PER_STEP // pages_per_block
    span = PAGES_PER_STEP * page

    @pl.when(st == 0)
    def _():
        qb_ref[...] = jnp.broadcast_to(qcol_ref[0], qb_ref.shape)
        gacc_ref[...] = jnp.zeros_like(gacc_ref)

    lane = lax.broadcasted_iota(I32, (N_HEADS, LANES), 1)
    raws = []
    for p in range(PAGES_PER_STEP):
        per_head = []
        for h in range(N_HEADS):
            prod = k_refs[p][0, h] * qb_ref[h]
            part = prod[0:SUBLANES]
            for c in range(1, HEAD_DIM // SUBLANES):
                part = part + prod[c * SUBLANES:(c + 1) * SUBLANES]
            per_head.append(part)
        raw = _sublane_sums(per_head)
        raws.append(raw)
        p_ref[0, st, :, p * page:(p + 1) * page] = raw * ATT_SCALE
    gacc = gacc_ref[...]
    for bi in range(bps):
        tot = raws[bi * pages_per_block]
        for r in range(1, pages_per_block):
            tot = tot + raws[bi * pages_per_block + r]
        gacc = jnp.where(lane == st * bps + bi, jnp.sum(tot, axis=-1, keepdims=True), gacc)
    gacc_ref[...] = gacc

    @pl.when(st == steps - 1)
    def _():
        gate = jnp.where(lane < nb, gacc * (1.0 / MOBA_BLOCK), -jnp.inf)
        sel = jnp.zeros((N_HEADS, LANES), F32)
        for _ in range(MOBA_TOPK):
            mx = jnp.max(gate, axis=-1, keepdims=True)
            pick = jnp.min(jnp.where(gate == mx, lane, LANES), axis=-1, keepdims=True)
            hit = lane == pick
            sel = jnp.where(hit, 1.0, sel)
            gate = jnp.where(hit, -jnp.inf, gate)
        n_keys = nb * MOBA_BLOCK
        expand = (lax.broadcasted_iota(I32, (LANES, n_keys), 1) // MOBA_BLOCK
                  == lax.broadcasted_iota(I32, (LANES, n_keys), 0))
        mask = _dot(sel, expand.astype(F32)) > 0.5
        masks = [mask[:, t * span:(t + 1) * span] for t in range(steps)]
        s_new = jnp.sum(_head_rows(q_ref[0]) * kn_ref[0], axis=-1, keepdims=True) * ATT_SCALE
        m = s_new
        for t in range(steps):
            m = jnp.maximum(m, jnp.max(jnp.where(masks[t], p_ref[0, t], NEG), axis=-1, keepdims=True))
        pn = jnp.exp(s_new - m)
        l = pn
        for t in range(steps):
            p = jnp.where(masks[t], jnp.exp(p_ref[0, t] - m), 0.0)
            l = l + jnp.sum(p, axis=-1, keepdims=True)
            p_ref[0, t] = p
        stat_ref[0] = jnp.where(lane == 0, l, jnp.where(lane == 1, pn, 0.0))
        need_ref[0] = jnp.max(sel, axis=0, keepdims=True).astype(I32)


def _page_specs(n_pages):
    def make(p):
        return pl.BlockSpec((1, N_HEADS, HEAD_DIM, LANES),
                            lambda b, s, pages, *_: (pages[b * n_pages + s * PAGES_PER_STEP + p], 0, 0, 0))
    return [make(p) for p in range(PAGES_PER_STEP)]


def _sample_scores(q, k_new, qcol, cache_kt, pt_flat, n_pages):
    db, w = q.shape
    page = cache_kt.shape[-1]
    ppb = MOBA_BLOCK // page
    steps = n_pages // PAGES_PER_STEP
    nb = n_pages // ppb
    span = PAGES_PER_STEP * page
    assert page == LANES and n_pages % PAGES_PER_STEP == 0 and PAGES_PER_STEP % ppb == 0 and MOBA_TOPK <= nb <= LANES
    one = lambda: pl.BlockSpec((1, 1, w), lambda b, s, pt: (b, 0, 0))
    grid_spec = pltpu.PrefetchScalarGridSpec(
        num_scalar_prefetch=1,
        grid=(db, steps),
        in_specs=[pl.BlockSpec((1, N_HEADS, HEAD_DIM, 1), lambda b, s, pt: (b, 0, 0, 0)), one(), one()]
        + _page_specs(n_pages),
        out_specs=[pl.BlockSpec((1, steps, N_HEADS, span), lambda b, s, pt: (b, 0, 0, 0)),
                   pl.BlockSpec((1, N_HEADS, LANES), lambda b, s, pt: (b, 0, 0)),
                   pl.BlockSpec((1, 1, LANES), lambda b, s, pt: (b, 0, 0))],
        scratch_shapes=[pltpu.VMEM((N_HEADS, HEAD_DIM, page), F32), pltpu.VMEM((N_HEADS, LANES), F32)],
    )
    r3 = lambda a: a.reshape(db, 1, w)
    return pl.pallas_call(
        functools.partial(_sample_scores_kernel, page=page, pages_per_block=ppb, nb=nb, steps=steps),
        grid_spec=grid_spec,
        out_shape=[jax.ShapeDtypeStruct((db, steps, N_HEADS, span), F32),
                   jax.ShapeDtypeStruct((db, N_HEADS, LANES), F32),
                   jax.ShapeDtypeStruct((db, 1, LANES), I32)],
        compiler_params=_cparams(("parallel", "arbitrary")),
        name="sample_scores",
    )(pt_flat, qcol, r3(q), r3(k_new), *([cache_kt] * PAGES_PER_STEP))


def _sample_attend_kernel(pages_ref, need_ref, vncol_ref, stat_ref, p_ref, *refs, page, pages_per_block, nb, steps):
    del pages_ref
    v_refs = refs[:PAGES_PER_STEP]
    o_ref, acc_ref = refs[PAGES_PER_STEP:]
    b = pl.program_id(0)
    st = pl.program_id(1)
    bps = PAGES_PER_STEP // pages_per_block

    @pl.when(st == 0)
    def _():
        acc_ref[...] = jnp.zeros_like(acc_ref)

    p_step = p_ref[0, 0]
    for bi in range(bps):
        @pl.when(need_ref[b * nb + st * bps + bi] != 0)
        def _(bi=bi):
            for h in range(N_HEADS):
                acc = acc_ref[h]
                for r in range(pages_per_block):
                    pg = bi * pages_per_block + r
                    acc = acc + v_refs[pg][0, h] * p_step[h:h + 1, pg * page:(pg + 1) * page]
                acc_ref[h] = acc

    @pl.when(st == steps - 1)
    def _():
        stat = stat_ref[0]
        for h in range(N_HEADS):
            tot = jnp.sum(acc_ref[h], axis=-1, keepdims=True)
            o_ref[0, h] = (tot + stat[h:h + 1, 1:2] * vncol_ref[0, h]) / stat[h:h + 1, 0:1]


def _fetched_pages(page_table, need, pages_per_block):
    db, n_pages = page_table.shape
    steps = n_pages // PAGES_PER_STEP
    needed = (jnp.repeat(need, pages_per_block, axis=1) != 0).reshape(db, steps, PAGES_PER_STEP)
    step_id = jnp.arange(steps, dtype=I32)[None, :, None]
    last = lax.cummax(jnp.where(needed, step_id, 0), axis=1)
    return jnp.take_along_axis(page_table.reshape(db, steps, PAGES_PER_STEP), last, axis=1).reshape(-1)


def _sample_attend(vncol, probs, stat, need, cache_vt, page_table):
    db, n_pages = page_table.shape
    page = cache_vt.shape[-1]
    ppb = MOBA_BLOCK // page
    steps = n_pages // PAGES_PER_STEP
    nb = n_pages // ppb
    span = PAGES_PER_STEP * page
    need2 = need.reshape(db, LANES)[:, :nb]
    col = lambda: pl.BlockSpec((1, N_HEADS, HEAD_DIM, 1), lambda b, s, *_: (b, 0, 0, 0))
    grid_spec = pltpu.PrefetchScalarGridSpec(
        num_scalar_prefetch=2,
        grid=(db, steps),
        in_specs=[col(),
                  pl.BlockSpec((1, N_HEADS, LANES), lambda b, s, *_: (b, 0, 0)),
                  pl.BlockSpec((1, 1, N_HEADS, span), lambda b, s, *_: (b, s, 0, 0))] + _page_specs(n_pages),
        out_specs=col(),
        scratch_shapes=[pltpu.VMEM((N_HEADS, HEAD_DIM, page), F32)],
    )
    out = pl.pallas_call(
        functools.partial(_sample_attend_kernel, page=page, pages_per_block=ppb, nb=nb, steps=steps),
        grid_spec=grid_spec,
        out_shape=jax.ShapeDtypeStruct((db, N_HEADS, HEAD_DIM, 1), F32),
        compiler_params=_cparams(("parallel", "arbitrary")),
        name="sample_attend",
    )(_fetched_pages(page_table, need2, ppb), need2.reshape(-1), vncol, stat, probs, *([cache_vt] * PAGES_PER_STEP))
    return out.reshape(db, N_HEADS * HEAD_DIM)


CONV_HALO = 32


def _ln_silu(y, g, b):
    mu = jnp.mean(y, axis=-1, keepdims=True)
    var = jnp.mean(jnp.square(y - mu), axis=-1, keepdims=True)
    n = (y - mu) * lax.rsqrt(var + EPS) * g + b
    return n * _sigmoid(n)


def _conv_prompt_kernel(zc_ref, zp_ref, dw_ref, db_ref, g_ref, b_ref, o_ref, win_ref, *, cw, tile):
    i = pl.program_id(1)
    prev = zp_ref[0, tile - CONV_HALO:tile, :]
    win_ref[0:CONV_HALO, :] = jnp.where(i > 0, prev, 0.0)
    win_ref[CONV_HALO:CONV_HALO + tile, :] = zc_ref[0]
    off = CONV_HALO - (cw - 1)
    y = jnp.broadcast_to(db_ref[...], (tile, db_ref.shape[-1]))
    for w in range(cw):
        y = y + win_ref[off + w:off + w + tile, :] * dw_ref[w:w + 1, :]
    o_ref[0] = _ln_silu(y, g_ref[...], b_ref[...])


def _conv_prompt(z, dw_w, dw_b, ln_g, ln_b, tile=256):
    b, s, c = z.shape
    cw = dw_w.shape[0]
    assert s % tile == 0 and cw - 1 <= CONV_HALO <= tile
    full = lambda a: pl.BlockSpec(a.shape, lambda bi, i: (0,) * a.ndim)
    return pl.pallas_call(
        functools.partial(_conv_prompt_kernel, cw=cw, tile=tile),
        grid=(b, s // tile),
        in_specs=[pl.BlockSpec((1, tile, c), lambda bi, i: (bi, i, 0)),
                  pl.BlockSpec((1, tile, c), lambda bi, i: (bi, jnp.maximum(i - 1, 0), 0)),
                  full(dw_w), full(dw_b), full(ln_g), full(ln_b)],
        out_specs=pl.BlockSpec((1, tile, c), lambda bi, i: (bi, i, 0)),
        out_shape=jax.ShapeDtypeStruct((b, s, c), F32),
        scratch_shapes=[pltpu.VMEM((CONV_HALO + tile, c), F32)],
        compiler_params=_cparams(("parallel", "arbitrary")),
        name="conv_prompt",
    )(z, z, dw_w, dw_b, ln_g, ln_b)


SAMPLE_CONV_ROWS = 8


def _conv_sample_kernel(st_ref, z_ref, dw_ref, db_ref, g_ref, b_ref, o_ref, ns_ref, *, cw):
    rows = []
    for r in range(SAMPLE_CONV_ROWS):
        y = jnp.sum(st_ref[r] * dw_ref[0:cw - 1, :], axis=0, keepdims=True)
        rows.append(y + z_ref[r:r + 1, :] * dw_ref[cw - 1:cw, :] + db_ref[...])
        ns_ref[r, 0:cw - 2, :] = st_ref[r, 1:cw - 1, :]
        ns_ref[r, cw - 2:cw - 1, :] = z_ref[r:r + 1, :]
    o_ref[...] = _ln_silu(jnp.concatenate(rows, axis=0), g_ref[...], b_ref[...])


def _conv_sample(state, z, dw_w, dw_b, ln_g, ln_b):
    db, sw, c = state.shape
    cw = dw_w.shape[0]
    r = SAMPLE_CONV_ROWS
    assert sw == cw - 1 and db % r == 0
    full = lambda a: pl.BlockSpec(a.shape, lambda i: (0,) * a.ndim)
    return pl.pallas_call(
        functools.partial(_conv_sample_kernel, cw=cw),
        grid=(db // r,),
        in_specs=[pl.BlockSpec((r, sw, c), lambda i: (i, 0, 0)), pl.BlockSpec((r, c), lambda i: (i, 0)),
                  full(dw_w), full(dw_b), full(ln_g), full(ln_b)],
        out_specs=[pl.BlockSpec((r, c), lambda i: (i, 0)), pl.BlockSpec((r, sw, c), lambda i: (i, 0, 0))],
        out_shape=[jax.ShapeDtypeStruct((db, c), F32), jax.ShapeDtypeStruct((db, sw, c), F32)],
        compiler_params=_cparams(("parallel",)),
        name="conv_sample",
    )(state, z, dw_w, dw_b, ln_g, ln_b)


def _outproj_kernel(x_ref, att_ref, conv_ref, sga_ref, sgb_ref, wao_ref, wco_ref, wo_ref, g2_ref, wq_ref,
                    x1_ref, hn_ref, pq_ref):
    merged = sga_ref[...] * _dot(att_ref[...], wao_ref[...]) + sgb_ref[...] * _dot(conv_ref[...], wco_ref[...])
    x1 = x_ref[...] + _dot(merged, wo_ref[...])
    x1_ref[...] = x1
    hn = _rms(x1, g2_ref[...])
    hn_ref[...] = hn
    pq_ref[...] = _dot(hn, wq_ref[...])


def _outproj(x2, att, conv, sga, sgb, wao, wco, wo, g2, wq, tm):
    t, d = x2.shape
    assert t % tm == 0
    row = lambda a: pl.BlockSpec((tm, a.shape[1]), lambda i: (i, 0))
    full = lambda a: pl.BlockSpec(a.shape, lambda i: (0,) * a.ndim)
    acts = (x2, att, conv, sga, sgb)
    wts = (wao, wco, wo, g2, wq)
    pq_w = wq.shape[1]
    return pl.pallas_call(
        _outproj_kernel,
        grid=(t // tm,),
        in_specs=[row(a) for a in acts] + [full(a) for a in wts],
        out_specs=[pl.BlockSpec((tm, d), lambda i: (i, 0)), pl.BlockSpec((tm, d), lambda i: (i, 0)),
                   pl.BlockSpec((tm, pq_w), lambda i: (i, 0))],
        out_shape=[jax.ShapeDtypeStruct((t, d), F32), jax.ShapeDtypeStruct((t, d), F32),
                   jax.ShapeDtypeStruct((t, pq_w), F32)],
        compiler_params=_cparams(("parallel",)),
        name="outproj",
    )(*acts, *wts)


PEER_TILE = 128
CAND_ROWS = PEER_TOPK + 8 * SUBLANES


def _peer_route_kernel(pq_ref, k12_ref, eidx_ref, pidx_ref, shift_ref, gate_ref, s_ref, c_ref, ci_ref, *, n_keys):
    kk = PEER_TOPK
    t = PEER_TILE
    nh = PEER_HEADS
    half = kk // 2
    assert kk == 16 and half == SUBLANES
    pair_span = n_keys * n_keys // 2
    key_id = lax.broadcasted_iota(I32, (n_keys, t), 0)
    kidx = lax.broadcasted_iota(I32, (kk, t), 0)
    for h in range(nh):
        qh = pq_ref[:, h * LANES:(h + 1) * LANES]
        s12 = _dot_nt(k12_ref[...], qh, HIGHEST)
        s_ref[2 * h] = s12[0:n_keys]
        s_ref[2 * h + 1] = s12[n_keys:2 * n_keys]

    def level1(it, carry):
        vals, idxs = carry
        nv, ni = [], []
        for c in range(2 * nh):
            s = s_ref[c]
            mx = jnp.max(s, axis=0, keepdims=True)
            pick = jnp.min(jnp.where(s == mx, key_id, n_keys), axis=0, keepdims=True)
            s_ref[c] = jnp.where(key_id == pick, -jnp.inf, s)
            nv.append(jnp.where(kidx == it, mx, vals[c]))
            ni.append(jnp.where(kidx == it, pick, idxs[c]))
        return tuple(nv), tuple(ni)

    zf = tuple(jnp.zeros((kk, t), F32) for _ in range(2 * nh))
    zi = tuple(jnp.zeros((kk, t), I32) for _ in range(2 * nh))
    vals, idxs = lax.fori_loop(0, kk, level1, (zf, zi))

    b16 = lax.broadcasted_iota(I32, (kk, t), 0)
    b8 = lax.broadcasted_iota(I32, (half, t), 0)
    flat_parts = [b16, kk + b8]
    for a in range(2, half):
        flat_parts.append(a * kk + b8)
    flat_parts.append((half + b8) * kk)
    flat = jnp.concatenate(flat_parts, axis=0)
    for h in range(nh):
        v1, v2, i1, i2 = vals[2 * h], vals[2 * h + 1], idxs[2 * h], idxs[2 * h + 1]
        c_parts = [v1[0:1] + v2, v1[1:2] + v2[0:half]]
        i_parts = [i1[0:1] * n_keys + i2, i1[1:2] * n_keys + i2[0:half]]
        for a in range(2, half):
            c_parts.append(jnp.where(b8 < kk // (a + 1), v1[a:a + 1] + v2[0:half], -jnp.inf))
            i_parts.append(i1[a:a + 1] * n_keys + i2[0:half])
        c_parts.append(v1[half:kk] + v2[0:1])
        i_parts.append(i1[half:kk] * n_keys + i2[0:1])
        c_ref[h] = jnp.concatenate(c_parts, axis=0)
        ci_ref[h] = jnp.concatenate(i_parts, axis=0)

    def level2(it, carry):
        vals2, exps = carry
        nv, ne = [], []
        for h in range(nh):
            c = c_ref[h]
            mx = jnp.max(c, axis=0, keepdims=True)
            pick = jnp.min(jnp.where(c == mx, flat, kk * kk), axis=0, keepdims=True)
            hit = flat == pick
            e = jnp.max(jnp.where(hit, ci_ref[h], -1), axis=0, keepdims=True)
            c_ref[h] = jnp.where(hit, -jnp.inf, c)
            nv.append(jnp.where(kidx == it, mx, vals2[h]))
            ne.append(jnp.where(kidx == it, e, exps[h]))
        return tuple(nv), tuple(ne)

    sc, eidx = lax.fori_loop(0, kk, level2, (zf[:nh], zi[:nh]))
    for h in range(nh):
        ex = jnp.exp(sc[h] - jnp.max(sc[h], axis=0, keepdims=True))
        rows = slice(h * kk, (h + 1) * kk)
        gate_ref[0, rows, :] = ex / jnp.sum(ex, axis=0, keepdims=True)
        eidx_ref[0, rows, :] = eidx[h]
        upper = (eidx[h] >= pair_span).astype(I32)
        pidx_ref[0, rows, :] = eidx[h] - upper * pair_span
        shift_ref[0, rows, :] = (1 - upper) * 16


def _peer_route(pq, k1ext, k2ext):
    t, w = pq.shape
    n_keys = k1ext.shape[0]
    slots = PEER_HEADS * PEER_TOPK
    assert t % PEER_TILE == 0 and w == PEER_HEADS * LANES and k1ext.shape[1] == LANES
    k12 = jnp.concatenate([k1ext, k2ext], axis=0)
    nt = t // PEER_TILE
    full = lambda a: pl.BlockSpec(a.shape, lambda i: (0,) * a.ndim)
    ospec = pl.BlockSpec((1, slots, PEER_TILE), lambda i: (i, 0, 0))
    ishape = jax.ShapeDtypeStruct((nt, slots, PEER_TILE), I32)
    return pl.pallas_call(
        functools.partial(_peer_route_kernel, n_keys=n_keys),
        grid=(nt,),
        in_specs=[pl.BlockSpec((PEER_TILE, w), lambda i: (i, 0)), full(k12)],
        out_specs=[ospec, ospec, ospec, ospec],
        out_shape=[ishape, ishape, ishape, jax.ShapeDtypeStruct((nt, slots, PEER_TILE), F32)],
        scratch_shapes=[pltpu.VMEM((2 * PEER_HEADS, n_keys, PEER_TILE), F32),
                        pltpu.VMEM((PEER_HEADS, CAND_ROWS, PEER_TILE), F32),
                        pltpu.VMEM((PEER_HEADS, CAND_ROWS, PEER_TILE), I32)],
        compiler_params=_cparams(("parallel",)),
        name="peer_route",
    )(pq, k12)


HI_MASK = np.uint32(0xFFFF0000)
HALF_SUB = SUBLANES // 2


def _as_f32(bits):
    return lax.bitcast_convert_type(bits, F32)


def _tile_copies(hbm_refs, smem_refs, sems, tile, slot, n):
    dst = pl.ds(pl.multiple_of(slot * n, n), n)
    return [pltpu.make_async_copy(h.at[tile], s.at[dst], sems.at[k, slot])
            for k, (h, s) in enumerate(zip(hbm_refs, smem_refs))]


def _stream_tiles(hbm_refs, smem_refs, sems, n_tiles, n):
    i = pl.program_id(0)
    slot = i % 2

    @pl.when(i == 0)
    def _():
        for c in _tile_copies(hbm_refs, smem_refs, sems, 0, 0, n):
            c.start()

    @pl.when(i + 1 < n_tiles)
    def _():
        for c in _tile_copies(hbm_refs, smem_refs, sems, i + 1, 1 - slot, n):
            c.start()

    for c in _tile_copies(hbm_refs, smem_refs, sems, i, slot, n):
        c.wait()
    return slot


SCORE_CHUNK_GROUPS = 16


def _peer_score_kernel(pidx_hbm, shift_hbm, x_ref, gate_ref, tab_ref, act_ref, pidx_sm, shift_sm, dots_ref, grp_ref, sems,
                       *, n_tiles, slots):
    tt = PEER_TILE
    n = slots * tt
    slot = _stream_tiles((pidx_hbm, shift_hbm), (pidx_sm, shift_sm), sems, n_tiles, n)
    lane = lax.broadcasted_iota(I32, (slots, tt), 1)
    dots_ref[...] = jnp.zeros_like(dots_ref)
    chunk_rows = SCORE_CHUNK_GROUPS * SUBLANES
    span = chunk_rows * tt
    view_len = 2 * n - (chunk_rows - 1) * tt

    lane_c = lax.broadcasted_iota(I32, (chunk_rows, tt), 1)
    grp_ref[...] = jnp.zeros_like(grp_ref)

    def finish(tok, buf, c):
        rows = pl.ds(pl.multiple_of(c * chunk_rows, chunk_rows), chunk_rows)
        part = grp_ref[buf, pl.ds(c * SCORE_CHUNK_GROUPS, SCORE_CHUNK_GROUPS)]
        col = jnp.sum(part.reshape(chunk_rows, LANES), axis=-1, keepdims=True)
        dots_ref[rows, :] = jnp.where(lane_c == tok, col, dots_ref[rows, :])

    def token(t, carry):
        base = slot * n + t
        x = x_ref[t]
        cur = t % 2

        def chunk(c, inner):
            finish(t - 1, 1 - cur, c)
            cbase = base + c * span
            for g in range(SCORE_CHUNK_GROUPS):
                prods = []
                for r in range(SUBLANES):
                    view = pl.ds((g * SUBLANES + r) * tt, view_len)
                    w = tab_ref[pidx_sm.at[view][cbase]]
                    prods.append(_as_f32((w << shift_sm.at[view][cbase].astype(U32)) & HI_MASK) * x)
                grp_ref[cur, c * SCORE_CHUNK_GROUPS + g] = _sublane_sums(prods)
            return inner

        lax.fori_loop(0, slots // chunk_rows, chunk, 0)
        return carry

    lax.fori_loop(0, tt, token, 0)
    for c in range(slots // chunk_rows):
        finish(tt - 1, (tt - 1) % 2, c)
    act_ref[0] = _gelu_erf(dots_ref[...]) * gate_ref[0]


def _peer_scores(x3, pidx, shift, gate, tab):
    nt, slots, tt = gate.shape
    n = slots * tt
    flat = lambda a: a.reshape(nt, n)
    kern = functools.partial(_peer_score_kernel, n_tiles=nt, slots=slots)
    return pl.pallas_call(
        kern,
        grid=(nt,),
        in_specs=[pl.BlockSpec(memory_space=pl.ANY), pl.BlockSpec(memory_space=pl.ANY),
                  pl.BlockSpec((tt, SUBLANES, LANES), lambda i: (i, 0, 0)),
                  pl.BlockSpec((1, slots, tt), lambda i: (i, 0, 0)),
                  pl.BlockSpec(memory_space=pltpu.VMEM)],
        out_specs=pl.BlockSpec((1, slots, tt), lambda i: (i, 0, 0)),
        out_shape=jax.ShapeDtypeStruct((nt, slots, tt), F32),
        scratch_shapes=[pltpu.SMEM((2 * n,), I32), pltpu.SMEM((2 * n,), I32),
                        pltpu.VMEM((slots, tt), F32), pltpu.VMEM((2, slots // SUBLANES, SUBLANES, LANES), F32),
                        pltpu.SemaphoreType.DMA((2, 2))],
        compiler_params=_cparams(("arbitrary",)),
        name="peer_scores",
    )(flat(pidx), flat(shift), x3, gate, tab)


MIX_ACCS = 4


MIX_CHUNK = 64


def _peer_mix_kernel(eidx_hbm, act_hbm, tab_ref, o_ref, eidx_sm, act_sm, sems, *, n_tiles, slots):
    tt = PEER_TILE
    n = slots * tt
    slot = _stream_tiles((eidx_hbm, act_hbm), (eidx_sm, act_sm), sems, n_tiles, n)
    span = MIX_CHUNK * tt
    view_len = 2 * n - (MIX_CHUNK - 1) * tt

    def token(t, carry):
        base = slot * n + t

        def chunk(c, accs):
            lo, hi = list(accs[0]), list(accs[1])
            cbase = base + c * span
            for r in range(MIX_CHUNK):
                view = pl.ds(r * tt, view_len)
                w = tab_ref[eidx_sm.at[view][cbase]]
                a = act_sm.at[view][cbase]
                lo[r % MIX_ACCS] = lo[r % MIX_ACCS] + a * _as_f32(w << 16)
                hi[r % MIX_ACCS] = hi[r % MIX_ACCS] + a * _as_f32(w & HI_MASK)
            return tuple(lo), tuple(hi)

        zero = tuple(jnp.zeros((HALF_SUB, LANES), F32) for _ in range(MIX_ACCS))
        lo, hi = lax.fori_loop(0, slots // MIX_CHUNK, chunk, (zero, zero))
        o_ref[t, 0:HALF_SUB, :] = (lo[0] + lo[1]) + (lo[2] + lo[3])
        o_ref[t, HALF_SUB:SUBLANES, :] = (hi[0] + hi[1]) + (hi[2] + hi[3])
        return carry

    lax.fori_loop(0, tt, token, 0)


def _peer_mix(eidx, act, tab):
    nt, slots, tt = act.shape
    n = slots * tt
    flat = lambda a: a.reshape(nt, n)
    kern = functools.partial(_peer_mix_kernel, n_tiles=nt, slots=slots)
    return pl.pallas_call(
        kern,
        grid=(nt,),
        in_specs=[pl.BlockSpec(memory_space=pl.ANY), pl.BlockSpec(memory_space=pl.ANY),
                  pl.BlockSpec(memory_space=pltpu.VMEM)],
        out_specs=pl.BlockSpec((tt, SUBLANES, LANES), lambda i: (i, 0, 0)),
        out_shape=jax.ShapeDtypeStruct((nt * tt, SUBLANES, LANES), F32),
        scratch_shapes=[pltpu.SMEM((2 * n,), I32), pltpu.SMEM((2 * n,), F32), pltpu.SemaphoreType.DMA((2, 2))],
        compiler_params=_cparams(("arbitrary",)),
        name="peer_mix",
    )(flat(eidx), flat(act), tab)


def _bf16_bits(tab):
    return lax.bitcast_convert_type(tab.astype(BF16), jnp.uint16).astype(U32)


def _pack_pairs(tab):
    n, d = tab.shape
    assert n % 2 == 0 and d == SUBLANES * LANES
    bits = _bf16_bits(tab)
    return (bits[:n // 2] | (bits[n // 2:] << 16)).reshape(n // 2, SUBLANES, LANES)


def _pack_halves(tab):
    n, d = tab.shape
    assert d == SUBLANES * LANES
    bits = _bf16_bits(tab)
    return (bits[:, :d // 2] | (bits[:, d // 2:] << 16)).reshape(n, HALF_SUB, LANES)


def _peer(hn, pq, k1ext, k2ext, u_pack, v_pack):
    t, d = hn.shape
    eidx, pidx, shift, gate = _peer_route(pq, k1ext, k2ext)
    act = _peer_scores(hn.reshape(t, SUBLANES, LANES), pidx, shift, gate, u_pack)
    return _peer_mix(eidx, act, v_pack).reshape(t, d)


def _ple_kernel(x1_ref, po_ref, pe_ref, g3_ref, wg_ref, wp_ref, gf_ref, y_ref, *, final):
    x2 = x1_ref[...] + po_ref[...]
    gate = _sigmoid(_dot(_rms(x2, g3_ref[...]), wg_ref[...]))
    x3 = x2 + gate * _dot(pe_ref[...], wp_ref[...])
    y_ref[...] = _rms(x3, gf_ref[...]) if final else x3


def _ple(x1, po, pe, g3, wg, wp, gf, tm, final):
    t, d = x1.shape
    assert t % tm == 0
    row = lambda a: pl.BlockSpec((tm, a.shape[1]), lambda i: (i, 0))
    full = lambda a: pl.BlockSpec(a.shape, lambda i: (0,) * a.ndim)
    return pl.pallas_call(
        functools.partial(_ple_kernel, final=final),
        grid=(t // tm,),
        in_specs=[row(x1), row(po), row(pe), full(g3), full(wg), full(wp), full(gf)],
        out_specs=pl.BlockSpec((tm, d), lambda i: (i, 0)),
        out_shape=jax.ShapeDtypeStruct((t, d), F32),
        compiler_params=_cparams(("parallel",)),
        name="ple_final",
    )(x1, po, pe, g3, wg, wp, gf)


def kernel(x_prompt, x_sample, cache_k, cache_v, state_conv, page_table, p_prompt, p_sample, norm1_g, w_in, dw_w, dw_b, cln_g, cln_b, w_att_out, w_conv_out, w_o, norm2_g, peer_wq, peer_k1, peer_k2, peer_u, peer_v, norm3_g, w_ple, w_ple_gate, final_g):
    depth = w_in.shape[0]
    b, s, d = x_prompt.shape
    db, ds, _ = x_sample.shape
    page = cache_k.shape[2]
    n_pages = page_table.shape[1]
    att_w = N_HEADS * HEAD_DIM
    assert ds == 1 and (n_pages * page) % MOBA_BLOCK == 0
    row = lambda a: a.reshape(1, -1)
    pt_flat = page_table.reshape(-1).astype(I32)
    gf = row(final_g)
    assert 2 * peer_k1.shape[-1] == LANES

    xp = x_prompt.reshape(b * s, d)
    xs = x_sample.reshape(db, d)
    outs = [[] for _ in range(6)]
    for li in range(depth):
        last = li == depth - 1
        bf = lambda a: a[li].astype(BF16)
        w_in_b, wao, wco, wo, wq, wpg, wpl = (bf(a) for a in (w_in, w_att_out, w_conv_out, w_o, peer_wq, w_ple_gate, w_ple))
        zeros = jnp.zeros_like(peer_k1[li])
        k1ext = jnp.concatenate([peer_k1[li], zeros], axis=1)
        k2ext = jnp.concatenate([zeros, peer_k2[li]], axis=1)
        u_pack = _pack_pairs(peer_u[li])
        v_pack = _pack_halves(peer_v[li])
        conv_w = (dw_w[li], row(dw_b[li]), row(cln_g[li]), row(cln_b[li]))
        g1, g2, g3 = row(norm1_g[li]), row(norm2_g[li]), row(norm3_g[li])
        ckt = jnp.transpose(cache_k[li], (0, 2, 3, 1))
        cvt = jnp.transpose(cache_v[li], (0, 2, 3, 1))

        qs, ks, vs, zs, sgas, sgbs = _inproj(xs, g1, w_in_b, db)
        col = lambda a: a.reshape(db, N_HEADS, HEAD_DIM, 1)
        probs, stat, need = _sample_scores(qs, ks, col(qs), ckt, pt_flat, n_pages)
        atts = _sample_attend(col(vs), probs, stat, need, cvt, page_table.astype(I32))
        convs, new_state = _conv_sample(state_conv[li], zs, *conv_w)
        x1s, hns, pqs = _outproj(xs, atts, convs, sgas, sgbs, wao, wco, wo, g2, wq, db)
        pos = _peer(hns, pqs, k1ext, k2ext, u_pack, v_pack)
        xs = _ple(x1s, pos, p_sample[li].reshape(db, -1), g3, wpg, wpl, gf, db, last)
        outs[3].append(ks.reshape(db, ds, N_HEADS, HEAD_DIM))
        outs[4].append(vs.reshape(db, ds, N_HEADS, HEAD_DIM))
        outs[5].append(new_state)

        q, k, v, z, sga, sgb, kt, vt = _inproj(xp, g1, w_in_b, 256, seq_len=s)
        att = _moba_prompt(q.reshape(b, s, att_w), k.reshape(b, s, att_w), v.reshape(b, s, att_w))
        z3 = z.reshape(b, s, -1)
        conv = _conv_prompt(z3, *conv_w)
        x1, hn, pq = _outproj(xp, att.reshape(b * s, att_w), conv.reshape(b * s, -1), sga, sgb, wao, wco, wo, g2, wq, 256)
        po = _peer(hn, pq, k1ext, k2ext, u_pack, v_pack)
        xp = _ple(x1, po, p_prompt[li].reshape(b * s, -1), g3, wpg, wpl, gf, 256, last)
        outs[0].append(jnp.transpose(kt, (0, 3, 1, 2)))
        outs[1].append(jnp.transpose(vt, (0, 3, 1, 2)))
        outs[2].append(z3[:, s - (dw_w.shape[1] - 1):, :])

    return (xp.reshape(b, s, d), xs.reshape(db, ds, d)) + tuple(jnp.stack(o) for o in outs)
```
